```python
import math
import jax, jax.numpy as jnp
from jax import lax
import numpy as np

D_MODEL = 2048
BATCH = 2
SEQ = 8192
DEPTH = 4

CTX_LEN = 256
GRID_W = 64
EPS = 1e-6
ROPE_BASE = 10000.0
Q_BLOCK = 128

DA_HEADS = 8
DA_DIM = 64
DA_QK_COLS = DA_HEADS * 2 * DA_DIM
DA_V_COLS = DA_HEADS * 2 * DA_DIM
DA_SCALE = DA_DIM ** -0.5
MLA_HEADS = 8
MLA_NOPE = 128
MLA_ROPE = 64
MLA_V = 128
MLA_Q_RANK = 512
MLA_KV_RANK = 256
MLA_SCALE = (MLA_NOPE + MLA_ROPE) ** -0.5
ATTN_IN = 2 * DA_QK_COLS + DA_V_COLS + MLA_Q_RANK + MLA_KV_RANK + MLA_ROPE
ATTN_CAT = DA_HEADS * 2 * DA_DIM + MLA_HEADS * MLA_V
SSD_INNER = 2 * D_MODEL
SSD_HEADDIM = 64
SSD_HEADS = SSD_INNER // SSD_HEADDIM
SSD_GROUPS = 8
SSD_STATE = 128
SSD_CONV = 5
SSD_CHUNK = 128
SSD_XBC = SSD_INNER + 2 * SSD_GROUPS * SSD_STATE
SSD_IN = SSD_INNER + SSD_XBC + 2 * SSD_HEADS
D_FF = 4 * D_MODEL
N_ATTN_LAYERS = (DEPTH + 1) // 2
N_SSD_LAYERS = DEPTH // 2

kernel_name = 'hybrid_diffattn_mla_ssd_dit_block'


def rmsnorm(x, g):
    x32 = x.astype(jnp.float32)
    y = x32 * lax.rsqrt(jnp.mean(x32 * x32, axis=-1, keepdims=True) + EPS)
    return (y * g.astype(jnp.float32)).astype(x.dtype)


def modulate(h, shift, scale):
    return h * (1 + scale) + shift


def split_cols(a, sizes):
    out, start = [], 0
    for s in sizes:
        out.append(a[..., start:start + s])
        start += s
    return out


def rope_2d(x, row, col):
    d = x.shape[-1]
    n = d // 4
    inv = ROPE_BASE ** (-jnp.arange(n, dtype=jnp.float32) / n)
    ang = jnp.concatenate([row[:, None] * inv, col[:, None] * inv], axis=-1)
    cos = jnp.cos(ang)[:, None, :].astype(x.dtype)
    sin = jnp.sin(ang)[:, None, :].astype(x.dtype)
    x1, x2 = x[..., :d // 2], x[..., d // 2:]
    return jnp.concatenate([x1 * cos - x2 * sin, x2 * cos + x1 * sin], axis=-1)


def over_query_blocks(f, *qs):
    bsz, s = qs[0].shape[:2]
    nb = s // Q_BLOCK
    blocks = tuple(jnp.moveaxis(q.reshape(bsz, nb, Q_BLOCK, *q.shape[2:]), 1, 0) for q in qs)
    out = lax.map(lambda b: f(*b), blocks)
    return jnp.moveaxis(out, 0, 1).reshape(bsz, s, out.shape[-1])


def diff_attend(q, k, v, lam):
    s = jnp.einsum('bqhcd,bkhcd->bhcqk', q, k, preferred_element_type=jnp.float32) * DA_SCALE
    p = jax.nn.softmax(s, axis=-1)
    w = p[:, :, 0] - lam * p[:, :, 1]
    return jnp.einsum('bhqk,bkhe->bqhe', w.astype(v.dtype), v)


def mla_attend(qn, qr, kn, kr, v):
    s = (jnp.einsum('bqhd,bkhd->bhqk', qn, kn, preferred_element_type=jnp.float32)
         + jnp.einsum('bqhd,bkd->bhqk', qr, kr, preferred_element_type=jnp.float32)) * MLA_SCALE
    p = jax.nn.softmax(s, axis=-1)
    return jnp.einsum('bhqk,bkhd->bqhd', p.astype(v.dtype), v)


def attention_mixer(h_lat, h_ctx, row, col, w_in, g_q, w_uq, g_kv, w_ukv,
                    lam_q1, lam_k1, lam_q2, lam_k2, g_sub, w_out, lambda_init, need_ctx):
    def project(h):
        bsz, t = h.shape[:2]
        qa, ka, va, cq, ckv, kr = split_cols(h @ w_in, [DA_QK_COLS, DA_QK_COLS, DA_V_COLS,
                                                         MLA_Q_RANK, MLA_KV_RANK, MLA_ROPE])
        qa = qa.reshape(bsz, t, DA_HEADS * 2, DA_DIM)
        ka = ka.reshape(bsz, t, DA_HEADS * 2, DA_DIM)
        va = va.reshape(bsz, t, DA_HEADS, 2 * DA_DIM)
        q = (rmsnorm(cq, g_q) @ w_uq).reshape(bsz, t, MLA_HEADS, MLA_NOPE + MLA_ROPE)
        kv = (rmsnorm(ckv, g_kv) @ w_ukv).reshape(bsz, t, MLA_HEADS, MLA_NOPE + MLA_V)
        return qa, ka, va, q[..., :MLA_NOPE], q[..., MLA_NOPE:], kv[..., :MLA_NOPE], kv[..., MLA_NOPE:], kr

    def to_pairs(a):
        return a.reshape(a.shape[0], a.shape[1], DA_HEADS, 2, DA_DIM)

    qa_c, ka_c, va_c, qn_c, qr_c, kn_c, vb_c, kr_c = project(h_ctx)
    qa_l, ka_l, va_l, qn_l, qr_l, kn_l, vb_l, kr_l = project(h_lat)
    qa_l, ka_l = rope_2d(qa_l, row, col), rope_2d(ka_l, row, col)
    qr_l = rope_2d(qr_l, row, col)
    kr_l = rope_2d(kr_l[:, :, None, :], row, col)[:, :, 0]

    lam = (jnp.exp(jnp.sum(lam_q1.astype(jnp.float32) * lam_k1.astype(jnp.float32)))
           - jnp.exp(jnp.sum(lam_q2.astype(jnp.float32) * lam_k2.astype(jnp.float32))) + lambda_init)

    def heads_out(qa, qn, qr, ka, va, kn, kr, vb):
        oa = diff_attend(qa, ka, va, lam)
        oa = rmsnorm(oa, g_sub) * (1.0 - lambda_init)
        ob = mla_attend(qn, qr, kn, kr, vb)
        return jnp.concatenate([oa.reshape(*oa.shape[:2], -1), ob.reshape(*ob.shape[:2], -1)], axis=-1)

    ka_all = jnp.concatenate([to_pairs(ka_c), to_pairs(ka_l)], axis=1)
    va_all = jnp.concatenate([va_c, va_l], axis=1)
    kn_all = jnp.concatenate([kn_c, kn_l], axis=1)
    kr_all = jnp.concatenate([kr_c, kr_l], axis=1)
    vb_all = jnp.concatenate([vb_c, vb_l], axis=1)
    o_lat = over_query_blocks(
        lambda qa, qn, qr: heads_out(qa, qn, qr, ka_all, va_all, kn_all, kr_all, vb_all),
        to_pairs(qa_l), qn_l, qr_l) @ w_out
    o_ctx = None
    if need_ctx:
        o_ctx = heads_out(to_pairs(qa_c), qn_c, qr_c, to_pairs(ka_c), va_c, kn_c, kr_c, vb_c) @ w_out
    return o_lat, o_ctx


def dwconv_centred(u, w, b):
    y = lax.conv_general_dilated(u, w[:, None, :].astype(u.dtype), window_strides=(1,),
                                 padding=[(SSD_CONV // 2, SSD_CONV // 2)],
                                 dimension_numbers=('NWC', 'WIO', 'NWC'), feature_group_count=u.shape[-1])
    return y + b


def ssd_scan(x, dt, A, Bm, Cm, h0):
    bsz, t, nh, hp = x.shape
    g, n = Bm.shape[2], Bm.shape[3]
    e = nh // g
    nc = t // SSD_CHUNK

    def chunks(a):
        a = a.astype(jnp.float32).reshape(bsz, nc, SSD_CHUNK, *a.shape[2:])
        return jnp.moveaxis(a, 1, 0)

    xs = chunks(x.reshape(bsz, t, g, e, hp))
    dts = chunks(dt.reshape(bsz, t, g, e))
    bs, cs = chunks(Bm), chunks(Cm)
    a_ge = A.astype(jnp.float32).reshape(g, e)
    tri = jnp.tril(jnp.ones((SSD_CHUNK, SSD_CHUNK), dtype=bool))[None, :, :, None, None]

    def step(h, inp):
        xc, dtc, bc, cc = inp
        a = jnp.cumsum(dtc * a_ge, axis=1)
        seg = jnp.exp(jnp.where(tri, a[:, :, None] - a[:, None, :], -jnp.inf))
        xdt = xc * dtc[..., None]
        cb = jnp.einsum('bign,bjgn->bijg', cc, bc)
        y = (jnp.einsum('bijg,bijge,bjgep->bigep', cb, seg, xdt)
             + jnp.einsum('bign,bgepn,bige->bigep', cc, h, jnp.exp(a)))
        tail = jnp.exp(a[:, -1:] - a)
        h = h * jnp.exp(a[:, -1])[..., None, None] + jnp.einsum('bjgn,bjge,bjgep->bgepn', bc, tail, xdt)
        return h, y

    h_final, ys = lax.scan(step, h0, (xs, dts, bs, cs))
    y = jnp.moveaxis(ys, 0, 1).reshape(bsz, t, nh, hp)
    return y.astype(x.dtype), h_final


def ssd_mixer(h_lat, h_ctx, w_in, conv_w, conv_b, dt_bias, a_log, d_skip, g_norm, w_out, need_ctx):
    def project(h):
        bsz, t = h.shape[:2]
        z, xbc, dt = split_cols(h @ w_in, [SSD_INNER, SSD_XBC, 2 * SSD_HEADS])
        xbc = jax.nn.silu(dwconv_centred(xbc, conv_w, conv_b))
        xs, bm, cm = split_cols(xbc, [SSD_INNER, SSD_GROUPS * SSD_STATE, SSD_GROUPS * SSD_STATE])
        xs = xs.reshape(bsz, t, SSD_HEADS, SSD_HEADDIM)
        bm = bm.reshape(bsz, t, SSD_GROUPS, SSD_STATE)
        cm = cm.reshape(bsz, t, SSD_GROUPS, SSD_STATE)
        dt = jax.nn.softplus(dt.astype(jnp.float32).reshape(bsz, t, 2, SSD_HEADS) + dt_bias.astype(jnp.float32))
        return z, xs, bm, cm, dt

    zc, xc, bc, cc, dtc = project(h_ctx)
    zl, xl, bl, cl, dtl = project(h_lat)
    A = -jnp.exp(a_log.astype(jnp.float32))
    bsz = h_lat.shape[0]
    h0 = jnp.zeros((bsz, SSD_GROUPS, SSD_HEADS // SSD_GROUPS, SSD_HEADDIM, SSD_STATE), jnp.float32)
    flip = lambda a: jnp.flip(a, axis=1)
    yc_f, hc_f = ssd_scan(xc, dtc[:, :, 0], A[0], bc, cc, h0)
    yl_f, _ = ssd_scan(xl, dtl[:, :, 0], A[0], bl, cl, hc_f)
    yc_b, hc_b = ssd_scan(flip(xc), flip(dtc[:, :, 1]), A[1], flip(bc), flip(cc), h0)
    yl_b, _ = ssd_scan(flip(xl), flip(dtl[:, :, 1]), A[1], flip(bl), flip(cl), hc_b)

    def finish(y, xs, z):
        b_, t = y.shape[:2]
        y = (y + d_skip[:, None] * xs).reshape(b_, t, SSD_INNER) * jax.nn.silu(z)
        y = rmsnorm(y.reshape(b_, t, SSD_GROUPS, SSD_INNER // SSD_GROUPS),
                    g_norm.reshape(SSD_GROUPS, SSD_INNER // SSD_GROUPS)).reshape(b_, t, SSD_INNER)
        return y @ w_out

    o_lat = finish(yl_f + flip(yl_b), xl, zl)
    o_ctx = finish(yc_f + flip(yc_b), xc, zc) if need_ctx else None
    return o_lat, o_ctx


def sqrelu_mlp(h, w1, w2):
    return jnp.square(jax.nn.relu(h @ w1)) @ w2


def setup_inputs(seed: int = 0) -> dict:
    key = jax.random.key(seed)
    ks = iter(jax.random.split(key, 64))
    f32 = jnp.float32
    nrm = lambda shape, scale: jax.random.normal(next(ks), shape, f32) * scale
    gain = lambda shape: 1.0 + 0.02 * jax.random.normal(next(ks), shape, f32)
    NA, NS, D = N_ATTN_LAYERS, N_SSD_LAYERS, D_MODEL
    dt0 = jnp.exp(jax.random.uniform(next(ks), (NS, 2, SSD_HEADS), f32, math.log(1e-3), math.log(1e-1)))
    return {
        'x': nrm((BATCH, SEQ, D), 1.0),
        'c': nrm((BATCH, D), 1.0),
        'ctx': nrm((BATCH, CTX_LEN, D), 1.0),
        'c_ctx': nrm((D,), 1.0),
        'mod_w': nrm((DEPTH, D, 6 * D), 0.5 * D ** -0.5),
        'mod_b': nrm((DEPTH, 6 * D), 0.01),
        'norm_mix': gain((DEPTH, D)),
        'norm_ffn': gain((DEPTH, D)),
        'ffn_w1': nrm((DEPTH, D, D_FF), D ** -0.5),
        'ffn_w2': nrm((DEPTH, D_FF, D), D_FF ** -0.5),
        'attn_w_in': nrm((NA, D, ATTN_IN), D ** -0.5),
        'mla_g_q': gain((NA, MLA_Q_RANK)),
        'mla_w_uq': nrm((NA, MLA_Q_RANK, MLA_HEADS * (MLA_NOPE + MLA_ROPE)), MLA_Q_RANK ** -0.5),
        'mla_g_kv': gain((NA, MLA_KV_RANK)),
        'mla_w_ukv': nrm((NA, MLA_KV_RANK, MLA_HEADS * (MLA_NOPE + MLA_V)), MLA_KV_RANK ** -0.5),
        'da_lam_q1': nrm((NA, DA_DIM), 0.1),
        'da_lam_k1': nrm((NA, DA_DIM), 0.1),
        'da_lam_q2': nrm((NA, DA_DIM), 0.1),
        'da_lam_k2': nrm((NA, DA_DIM), 0.1),
        'da_g_sub': gain((NA, 2 * DA_DIM)),
        'attn_w_out': nrm((NA, ATTN_CAT, D), ATTN_CAT ** -0.5),
        'ssd_w_in': nrm((NS, D, SSD_IN), D ** -0.5),
        'ssd_conv_w': nrm((NS, SSD_CONV, SSD_XBC), SSD_CONV ** -0.5),
        'ssd_conv_b': nrm((NS, SSD_XBC), 0.02),
        'ssd_dt_bias': dt0 + jnp.log(-jnp.expm1(-dt0)),
        'ssd_a_log': jnp.log(jax.random.uniform(next(ks), (NS, 2, SSD_HEADS), f32, 1.0, 16.0)),
        'ssd_d': 1.0 + 0.1 * jax.random.normal(next(ks), (NS, SSD_HEADS), f32),
        'ssd_g_norm': gain((NS, SSD_INNER)),
        'ssd_w_out': nrm((NS, SSD_INNER, D), SSD_INNER ** -0.5),
        'final_g': gain((D,)),
    }


def reference(x, c, ctx, c_ctx, mod_w, mod_b, norm_mix, norm_ffn, ffn_w1, ffn_w2,
              attn_w_in, mla_g_q, mla_w_uq, mla_g_kv, mla_w_ukv,
              da_lam_q1, da_lam_k1, da_lam_q2, da_lam_k2, da_g_sub, attn_w_out,
              ssd_w_in, ssd_conv_w, ssd_conv_b, ssd_dt_bias, ssd_a_log, ssd_d, ssd_g_norm, ssd_w_out,
              final_g):
    s = x.shape[1]
    ROWS = s // GRID_W
    row = jnp.broadcast_to(jnp.arange(ROWS, dtype=jnp.float32)[:, None], (ROWS, GRID_W)).reshape(s)
    col = jnp.broadcast_to(jnp.arange(GRID_W, dtype=jnp.float32)[None, :], (ROWS, GRID_W)).reshape(s)
    s_lat = jax.nn.silu(c)
    s_ctx = jax.nn.silu(c_ctx)
    x_lat, x_ctx = x, ctx
    for l in range(DEPTH):
        need_ctx = l < DEPTH - 1
        m_lat = (s_lat @ mod_w[l] + mod_b[l]).reshape(-1, 6, D_MODEL)[:, :, None, :]
        m_ctx = (s_ctx @ mod_w[l] + mod_b[l]).reshape(6, D_MODEL)
        h_lat = modulate(rmsnorm(x_lat, norm_mix[l]), m_lat[:, 0], m_lat[:, 1])
        h_ctx = modulate(rmsnorm(x_ctx, norm_mix[l]), m_ctx[0], m_ctx[1])
        i = l // 2
        if l % 2 == 0:
            lambda_init = 0.8 - 0.6 * math.exp(-0.3 * l)
            o_lat, o_ctx = attention_mixer(h_lat, h_ctx, row, col, attn_w_in[i], mla_g_q[i], mla_w_uq[i],
                                           mla_g_kv[i], mla_w_ukv[i], da_lam_q1[i], da_lam_k1[i],
                                           da_lam_q2[i], da_lam_k2[i], da_g_sub[i], attn_w_out[i],
                                           lambda_init, need_ctx)
        else:
            o_lat, o_ctx = ssd_mixer(h_lat, h_ctx, ssd_w_in[i], ssd_conv_w[i], ssd_conv_b[i], ssd_dt_bias[i],
                                     ssd_a_log[i], ssd_d[i], ssd_g_norm[i], ssd_w_out[i], need_ctx)
        x_lat = x_lat + m_lat[:, 2] * o_lat
        h_lat = modulate(rmsnorm(x_lat, norm_ffn[l]), m_lat[:, 3], m_lat[:, 4])
        x_lat = x_lat + m_lat[:, 5] * sqrelu_mlp(h_lat, ffn_w1[l], ffn_w2[l])
        if need_ctx:
            x_ctx = x_ctx + m_ctx[2] * o_ctx
            h_ctx = modulate(rmsnorm(x_ctx, norm_ffn[l]), m_ctx[3], m_ctx[4])
            x_ctx = x_ctx + m_ctx[5] * sqrelu_mlp(h_ctx, ffn_w1[l], ffn_w2[l])
    return rmsnorm(x_lat, final_g)
```

```python
import functools
import math

import jax
import jax.numpy as jnp
from jax import lax
from jax.experimental import pallas as pl
from jax.experimental.pallas import tpu as pltpu

F32 = jnp.float32
BF16 = jnp.bfloat16

EPS = 1e-6
ROPE_BASE = 10000.0
GRID_W = 64
LOG2E = 1.4426950408889634

DA_HEADS = 8
DA_DIM = 64
DA_COLS = DA_HEADS * 2 * DA_DIM
MLA_HEADS = 8
MLA_NOPE = 128
MLA_ROPE = 64
MLA_V = 128
MLA_Q_RANK = 512
MLA_KV_RANK = 256
MLA_QK_PAD = 256
SSD_HEADDIM = 64
SSD_GROUPS = 8
SSD_STATE = 128
SSD_CONV = 5
SSD_CHUNK = 128

LANE = 128
ROW_TILE = 512
ATT_TILE = 256
CONV_ROWS = 256
HALO = 16
MOD_ROWS = 8
VMEM_LIMIT = 56 * 1024 * 1024


def _silu(x):
    return x * (1.0 / (1.0 + jnp.exp(-x)))


def _params(sem):
    return pltpu.CompilerParams(dimension_semantics=sem, vmem_limit_bytes=VMEM_LIMIT)


def _mod_kernel(c_ref, w_ref, b_ref, o_ref):
    s = _silu(c_ref[...]).astype(BF16)
    o_ref[0] = jnp.dot(s, w_ref[0].astype(BF16), preferred_element_type=F32) + b_ref[0]


def _modulation(cvec, mod_w, mod_b):
    depth, d, _ = mod_w.shape
    tn = min(1024, d)
    nj = d // tn
    return pl.pallas_call(
        _mod_kernel,
        grid=(depth, 6, nj),
        in_specs=[
            pl.BlockSpec((MOD_ROWS, d), lambda l, r, j: (0, 0)),
            pl.BlockSpec((1, d, tn), lambda l, r, j: (l, 0, r * nj + j)),
            pl.BlockSpec((1, 1, tn), lambda l, r, j: (l * 6 + r, 0, j)),
        ],
        out_specs=pl.BlockSpec((1, MOD_ROWS, tn), lambda l, r, j: (l * 6 + r, 0, j)),
        out_shape=jax.ShapeDtypeStruct((depth * 6, MOD_ROWS, d), F32),
        compiler_params=_params(("arbitrary", "arbitrary", "arbitrary")),
        name="modulation",
    )(cvec, mod_w, mod_b.reshape(depth * 6, 1, d))


def _linear_kernel(*refs, pre, pre_mod, epi, tiles_per_seq, n_batch, tn, period):
    it = iter(refs)
    x_ref, w_ref = next(it), next(it)
    g_ref = next(it) if pre else None
    sh_ref = next(it) if pre_mod else None
    sc_ref = next(it) if pre_mod else None
    if epi == "softplus":
        bias_ref = next(it)
    elif epi == "rope":
        cos_ref, sin_ref = next(it), next(it)
    elif epi == "gate_res":
        res_ref, gate_ref = next(it), next(it)
    o_ref = next(it)
    h_ref = next(it) if pre else None

    grp = jnp.minimum(pl.program_id(0) // tiles_per_seq, n_batch)

    if pre:
        @pl.when(pl.program_id(1) == 0)
        def _():
            x32 = x_ref[...].astype(F32)
            ms = jnp.mean(x32 * x32, axis=-1, keepdims=True)
            y = x32 * lax.rsqrt(ms + EPS) * g_ref[...]
            if pre_mod:
                y = y * (1.0 + sc_ref[0, pl.ds(grp, 1), :]) + sh_ref[0, pl.ds(grp, 1), :]
            h_ref[...] = y.astype(BF16)
        lhs = h_ref[...]
    else:
        lhs = x_ref[...]

    acc = jnp.dot(lhs, w_ref[...], preferred_element_type=F32)

    if epi == "none":
        o_ref[...] = acc.astype(o_ref.dtype)
    elif epi == "relu2":
        r = jnp.maximum(acc, 0.0)
        o_ref[...] = (r * r).astype(o_ref.dtype)
    elif epi == "softplus":
        v = acc + bias_ref[...]
        o_ref[...] = (jnp.maximum(v, 0.0) + jnp.log(1.0 + jnp.exp(-jnp.abs(v)))).astype(o_ref.dtype)
    elif epi == "rope":
        cos, sin = cos_ref[...], sin_ref[...]
        lane = lax.broadcasted_iota(jnp.int32, cos.shape, 1)
        first = (lane % (2 * 32)) < 32
        for s in range(tn // period):
            a = acc[:, s * period:(s + 1) * period]
            partner = jnp.where(first, pltpu.roll(a, period - 32, 1), pltpu.roll(a, 32, 1))
            o_ref[:, s * period:(s + 1) * period] = (a * cos + partner * sin).astype(o_ref.dtype)
    elif epi == "gate_res":
        o_ref[...] = res_ref[...] + gate_ref[0, pl.ds(grp, 1), :] * acc


def _linear(x, w, *, tn, out_dtype, lay, name, k=None, x_col_block=0, m_tiles=None,
            pre_gain=None, pre_mod=None, epi="none", epi_args=()):
    m = x.shape[0]
    k = x.shape[1] if k is None else k
    n = w.shape[1]
    tm = ROW_TILE
    mt = m // tm if m_tiles is None else m_tiles
    nt = n // tn
    assert n % tn == 0 and w.shape[0] == k
    pre = pre_gain is not None
    period = None

    args = [x, w]
    specs = [pl.BlockSpec((tm, k), lambda i, j: (i, x_col_block)),
             pl.BlockSpec((k, tn), lambda i, j: (0, j))]
    if pre:
        args.append(pre_gain)
        specs.append(pl.BlockSpec((1, k), lambda i, j: (0, 0)))
    if pre_mod is not None:
        mods, shift_idx, scale_idx = pre_mod
        args += [mods, mods]
        specs += [pl.BlockSpec((1, MOD_ROWS, k), lambda i, j: (shift_idx, 0, 0)),
                  pl.BlockSpec((1, MOD_ROWS, k), lambda i, j: (scale_idx, 0, 0))]
    if epi == "softplus":
        (bias,) = epi_args
        args.append(bias)
        specs.append(pl.BlockSpec((1, tn), lambda i, j: (0, j)))
    elif epi == "rope":
        cos, sin = epi_args
        period = cos.shape[1]
        n_lat, tps = lay["lat_tiles"], lay["tiles_per_seq"]
        tab_map = lambda i, j: (jnp.where(i < n_lat, i % tps, tps), 0)
        args += [cos, sin]
        specs += [pl.BlockSpec((tm, period), tab_map), pl.BlockSpec((tm, period), tab_map)]
    elif epi == "gate_res":
        res, mods, gate_idx = epi_args
        args += [res, mods]
        specs += [pl.BlockSpec((tm, tn), lambda i, j: (i, j)),
                  pl.BlockSpec((1, MOD_ROWS, tn), lambda i, j: (gate_idx, 0, j))]

    kern = functools.partial(_linear_kernel, pre=pre, pre_mod=pre_mod is not None, epi=epi,
                             tiles_per_seq=lay["tiles_per_seq"], n_batch=lay["batch"], tn=tn, period=period)
    return pl.pallas_call(
        kern,
        grid=(mt, nt),
        in_specs=specs,
        out_specs=pl.BlockSpec((tm, tn), lambda i, j: (i, j)),
        out_shape=jax.ShapeDtypeStruct((mt * tm, n), out_dtype),
        scratch_shapes=[pltpu.VMEM((tm, k), BF16)] if pre else [],
        compiler_params=_params(("arbitrary", "arbitrary")),
        name=name,
    )(*args)


def _attn_kernel(lam_ref, gsub_ref, qa_ref, ka_ref, va_ref, qm_ref, km_ref, vm_ref, o_ref,
                 qsel_ref, m_ref, l_ref, acc_ref, *, lambda_init, n_kv, n_ctx_kv, n_lat_q):
    qi, t = pl.program_id(0), pl.program_id(1)
    tk = ka_ref.shape[0]

    @pl.when(t == 0)
    def _init():
        m_ref[...] = jnp.full(m_ref.shape, -jnp.inf, F32)
        l_ref[...] = jnp.zeros(l_ref.shape, F32)
        acc_ref[...] = jnp.zeros(acc_ref.shape, F32)
        qa = qa_ref[...]
        lane = lax.broadcasted_iota(jnp.int32, qa.shape, 1)
        first = (lane % LANE) < DA_DIM
        zero = jnp.zeros_like(qa)
        qsel_ref[0] = jnp.where(first, qa, zero)
        qsel_ref[1] = jnp.where(first, zero, qa)

    def update(idx, q, k, v):
        s = lax.dot_general(q, k, (((1,), (1,)), ((), ())), preferred_element_type=F32)
        m_prev = m_ref[idx]
        m_next = jnp.maximum(m_prev, jnp.max(s, axis=1, keepdims=True))
        alpha = jnp.exp2(m_prev - m_next)
        p = jnp.exp2(s - jnp.concatenate([m_next] * (tk // LANE), axis=1))
        l_ref[idx] = alpha * l_ref[idx] + jnp.sum(p, axis=1, keepdims=True)
        acc_ref[idx] = alpha * acc_ref[idx] + jnp.dot(p.astype(BF16), v, preferred_element_type=F32)
        m_ref[idx] = m_next

    @pl.when(jnp.logical_or(qi < n_lat_q, t < n_ctx_kv))
    def _step():
        for h in range(DA_HEADS):
            k = ka_ref[:, h * LANE:(h + 1) * LANE]
            v = va_ref[:, h * LANE:(h + 1) * LANE]
            for c in range(2):
                update(2 * h + c, qsel_ref[c, :, h * LANE:(h + 1) * LANE], k, v)
        for h in range(MLA_HEADS):
            update(2 * DA_HEADS + h,
                   qm_ref[:, h * MLA_QK_PAD:(h + 1) * MLA_QK_PAD],
                   km_ref[:, h * MLA_QK_PAD:(h + 1) * MLA_QK_PAD],
                   vm_ref[:, h * MLA_V:(h + 1) * MLA_V])

    @pl.when(t == n_kv - 1)
    def _finish():
        lv = lam_ref[...]
        lam = (jnp.exp(jnp.sum(lv[0:1] * lv[1:2], axis=1, keepdims=True))
               - jnp.exp(jnp.sum(lv[2:3] * lv[3:4], axis=1, keepdims=True)) + lambda_init)
        for h in range(DA_HEADS):
            o = acc_ref[2 * h] / l_ref[2 * h] - lam * (acc_ref[2 * h + 1] / l_ref[2 * h + 1])
            ms = jnp.mean(o * o, axis=1, keepdims=True)
            o = o * lax.rsqrt(ms + EPS) * gsub_ref[...] * (1.0 - lambda_init)
            o_ref[:, h * LANE:(h + 1) * LANE] = o.astype(o_ref.dtype)
        for h in range(MLA_HEADS):
            idx = 2 * DA_HEADS + h
            o_ref[:, DA_COLS + h * MLA_V:DA_COLS + (h + 1) * MLA_V] = (acc_ref[idx] / l_ref[idx]).astype(o_ref.dtype)


def _attention(qk_a, v_a, q_m, k_m, kv_m, lam_vecs, g_sub, *, lay, lambda_init):
    m = qk_a.shape[0]
    t = ATT_TILE
    s_len, ctx_len, bsz = lay["seq"], lay["ctx"], lay["batch"]
    n_lat_q = bsz * s_len // t
    n_ctx_q = bsz * ctx_len // t
    n_ctx_kv = ctx_len // t
    n_lat_kv = s_len // t
    n_kv = n_ctx_kv + n_lat_kv
    ctx_base = bsz * s_len // t

    def batch_of(qi):
        return jnp.where(qi < n_lat_q, qi // n_lat_kv, (qi - n_lat_q) // n_ctx_kv)

    def kv_block(qi, tt):
        b = batch_of(qi)
        ctx_blk = ctx_base + b * n_ctx_kv + jnp.minimum(tt, n_ctx_kv - 1)
        lat_blk = b * n_lat_kv + (tt - n_ctx_kv)
        use_ctx = jnp.logical_or(tt < n_ctx_kv, qi >= n_lat_q)
        return jnp.where(use_ctx, ctx_blk, lat_blk)

    n_state = 2 * DA_HEADS + MLA_HEADS
    kern = functools.partial(_attn_kernel, lambda_init=lambda_init, n_kv=n_kv, n_ctx_kv=n_ctx_kv, n_lat_q=n_lat_q)
    return pl.pallas_call(
        kern,
        grid=(n_lat_q + n_ctx_q, n_kv),
        in_specs=[
            pl.BlockSpec((4, DA_DIM), lambda qi, tt: (0, 0)),
            pl.BlockSpec((1, 2 * DA_DIM), lambda qi, tt: (0, 0)),
            pl.BlockSpec((t, DA_COLS), lambda qi, tt: (qi, 0)),
            pl.BlockSpec((t, DA_COLS), lambda qi, tt: (kv_block(qi, tt), 1)),
            pl.BlockSpec((t, DA_COLS), lambda qi, tt: (kv_block(qi, tt), 0)),
            pl.BlockSpec((t, MLA_HEADS * MLA_QK_PAD), lambda qi, tt: (qi, 0)),
            pl.BlockSpec((t, MLA_HEADS * MLA_QK_PAD), lambda qi, tt: (kv_block(qi, tt), 0)),
            pl.BlockSpec((t, MLA_HEADS * MLA_V), lambda qi, tt: (kv_block(qi, tt), 1)),
        ],
        out_specs=pl.BlockSpec((t, DA_COLS + MLA_HEADS * MLA_V), lambda qi, tt: (qi, 0)),
        out_shape=jax.ShapeDtypeStruct((m, DA_COLS + MLA_HEADS * MLA_V), BF16),
        scratch_shapes=[
            pltpu.VMEM((2, t, DA_COLS), BF16),
            pltpu.VMEM((n_state, t, LANE), F32),
            pltpu.VMEM((n_state, t, LANE), F32),
            pltpu.VMEM((n_state, t, LANE), F32),
        ],
        compiler_params=_params(("arbitrary", "arbitrary")),
        name="attention",
    )(lam_vecs, g_sub, qk_a, qk_a, v_a, q_m, k_m, kv_m)


def _conv_kernel(xm_ref, xp_ref, xn_ref, w_ref, b_ref, o_ref, scr_ref, *, tm, seq, ctx, n_lat_rows):
    row0 = pl.program_id(0) * tm
    in_lat = row0 < n_lat_rows
    local = jnp.where(in_lat, row0 % seq, (row0 - n_lat_rows) % ctx)
    seq_len = jnp.where(in_lat, seq, ctx)
    keep_prev = (local != 0).astype(F32)
    keep_next = (local + tm != seq_len).astype(F32)
    scr_ref[0:HALO, :] = xp_ref[...].astype(F32) * keep_prev
    scr_ref[HALO:HALO + tm, :] = xm_ref[...].astype(F32)
    scr_ref[HALO + tm:2 * HALO + tm, :] = xn_ref[...].astype(F32) * keep_next
    acc = jnp.broadcast_to(b_ref[...], (tm, b_ref.shape[1]))
    for kk in range(SSD_CONV):
        acc = acc + w_ref[kk:kk + 1, :] * scr_ref[pl.ds(HALO - SSD_CONV // 2 + kk, tm), :]
    o_ref[...] = _silu(acc).astype(o_ref.dtype)


def _ssd_conv(zx, conv_w, conv_b, *, lay, col0):
    m = zx.shape[0]
    c = conv_w.shape[1]
    tm, tc = CONV_ROWS, 1024
    hb = tm // HALO
    last = m // HALO - 1
    cb0 = col0 // tc
    kern = functools.partial(_conv_kernel, tm=tm, seq=lay["seq"], ctx=lay["ctx"], n_lat_rows=lay["batch"] * lay["seq"])
    return pl.pallas_call(
        kern,
        grid=(m // tm, c // tc),
        in_specs=[
            pl.BlockSpec((tm, tc), lambda i, j: (i, cb0 + j)),
            pl.BlockSpec((HALO, tc), lambda i, j: (jnp.maximum(i * hb - 1, 0), cb0 + j)),
            pl.BlockSpec((HALO, tc), lambda i, j: (jnp.minimum((i + 1) * hb, last), cb0 + j)),
            pl.BlockSpec((SSD_CONV, tc), lambda i, j: (0, j)),
            pl.BlockSpec((1, tc), lambda i, j: (0, j)),
        ],
        out_specs=pl.BlockSpec((tm, tc), lambda i, j: (i, j)),
        out_shape=jax.ShapeDtypeStruct((m, c), BF16),
        scratch_shapes=[pltpu.VMEM((tm + 2 * HALO, tc), F32)],
        compiler_params=_params(("arbitrary", "arbitrary")),
        name="ssd_conv",
    )(zx, zx, zx, conv_w, conv_b)


def _ssd_scan_kernel(*refs, direction, finish, n_heads):
    it = iter(refs)
    alog_ref, x_ref, b_ref, c_ref, dt_ref = (next(it) for _ in range(5))
    if finish:
        yf_ref, z_ref, dsk_ref, gn_ref = (next(it) for _ in range(4))
    o_ref = next(it)
    state_ref = next(it)
    y_ref = next(it) if finish else o_ref

    L, N = SSD_CHUNK, SSD_STATE
    heads_per_group = n_heads // SSD_GROUPS
    pairs_per_group = heads_per_group // 2

    @pl.when(pl.program_id(1) == 0)
    def _():
        state_ref[...] = jnp.zeros(state_ref.shape, F32)

    a_rate = -jnp.exp(alog_ref[...])
    dt = dt_ref[...]
    ri = lax.broadcasted_iota(jnp.int32, (L, L), 0)
    ci = lax.broadcasted_iota(jnp.int32, (L, L), 1)
    tri = (ri >= ci) if direction == 0 else (ri <= ci)
    a = jnp.dot(tri.astype(F32), dt * a_rate, preferred_element_type=F32, precision=lax.Precision.HIGHEST)
    a_t = a.T
    dt_t = dt.T
    last = L - 1 if direction == 0 else 0
    lane = lax.broadcasted_iota(jnp.int32, (1, LANE), 1)
    first = lane < SSD_HEADDIM

    for g in range(SSD_GROUPS):
        bg = b_ref[:, g * N:(g + 1) * N]
        cg = c_ref[:, g * N:(g + 1) * N]
        cb = lax.dot_general(cg, bg, (((1,), (1,)), ((), ())), preferred_element_type=F32)
        bg32, cg32 = bg.astype(F32), cg.astype(F32)
        for pp in range(pairs_per_group):
            p = g * pairs_per_group + pp
            xp = x_ref[:, p * LANE:(p + 1) * LANE]
            h_t = state_ref[p]
            lhs, bw, dec = [], [], []
            for kk in range(2):
                c = direction * n_heads + 2 * p + kk
                a_col, a_row = a[:, c:c + 1], a_t[c:c + 1, :]
                tot = a[last:last + 1, c:c + 1]
                seg = jnp.exp(jnp.where(tri, a_col - a_row, -jnp.inf))
                lhs.append((cb * seg * dt_t[c:c + 1, :]).astype(BF16))
                lhs.append((cg32 * jnp.exp(a_col)).astype(BF16))
                bw.append((bg32 * (jnp.exp(tot - a_col) * dt[:, c:c + 1])).astype(BF16))
                dec.append(jnp.exp(tot))
            zero_x = jnp.zeros_like(xp)
            h_b = h_t.astype(BF16)
            zero_h = jnp.zeros_like(h_b)
            rhs = jnp.concatenate([jnp.where(first, xp, zero_x), jnp.where(first, h_b, zero_h),
                                   jnp.where(first, zero_x, xp), jnp.where(first, zero_h, h_b)], axis=0)
            y = jnp.dot(jnp.concatenate(lhs, axis=1), rhs, preferred_element_type=F32)
            y_ref[:, p * LANE:(p + 1) * LANE] = y.astype(y_ref.dtype)
            r = lax.dot_general(jnp.concatenate(bw, axis=1), xp, (((0,), (0,)), ((), ())),
                                preferred_element_type=F32)
            state_ref[p] = h_t * jnp.where(first, dec[0], dec[1]) + jnp.where(first, r[:N], r[N:])

    if finish:
        gw = n_heads * SSD_HEADDIM // SSD_GROUPS
        for g in range(SSD_GROUPS):
            sl = slice(g * gw, (g + 1) * gw)
            v = yf_ref[:, sl] + y_ref[:, sl] + dsk_ref[:, sl] * x_ref[:, sl].astype(F32)
            v = v * _silu(z_ref[:, sl].astype(F32))
            ms = jnp.mean(v * v, axis=1, keepdims=True)
            o_ref[:, sl] = (v * lax.rsqrt(ms + EPS) * gn_ref[:, sl]).astype(o_ref.dtype)


def _ssd_scan(xbc, dt, a_log, *, lay, direction, finish_args=None):
    m = xbc.shape[0]
    L = SSD_CHUNK
    gn = SSD_GROUPS * SSD_STATE
    inner = xbc.shape[1] - 2 * gn
    n_heads = inner // SSD_HEADDIM
    bsz = lay["batch"]
    ncl, ncc = lay["seq"] // L, lay["ctx"] // L
    ctx_base = bsz * ncl
    finish = finish_args is not None

    def chunk(b, t):
        if direction == 0:
            return jnp.where(t < ncc, ctx_base + b * ncc + t, b * ncl + (t - ncc))
        return jnp.where(t < ncc, ctx_base + b * ncc + (ncc - 1 - t), b * ncl + (ncl - 1 - (t - ncc)))

    row = lambda b, t: (chunk(b, t), 0)
    args = [a_log, xbc, xbc, xbc, dt]
    specs = [
        pl.BlockSpec((1, 2 * n_heads), lambda b, t: (0, 0)),
        pl.BlockSpec((L, inner), row),
        pl.BlockSpec((L, gn), lambda b, t: (chunk(b, t), inner // gn)),
        pl.BlockSpec((L, gn), lambda b, t: (chunk(b, t), inner // gn + 1)),
        pl.BlockSpec((L, 2 * n_heads), row),
    ]
    scratch = [pltpu.VMEM((n_heads // 2, SSD_STATE, LANE), F32)]
    if finish:
        y_f, zx, dsk, g_norm = finish_args
        args += [y_f, zx, dsk, g_norm]
        specs += [pl.BlockSpec((L, inner), row), pl.BlockSpec((L, inner), row),
                  pl.BlockSpec((1, inner), lambda b, t: (0, 0)), pl.BlockSpec((1, inner), lambda b, t: (0, 0))]
        scratch.append(pltpu.VMEM((L, inner), F32))
    kern = functools.partial(_ssd_scan_kernel, direction=direction, finish=finish, n_heads=n_heads)
    return pl.pallas_call(
        kern,
        grid=(bsz, ncc + ncl),
        in_specs=specs,
        out_specs=pl.BlockSpec((L, inner), row),
        out_shape=jax.ShapeDtypeStruct((m, inner), BF16 if finish else F32),
        scratch_shapes=scratch,
        compiler_params=_params(("arbitrary", "arbitrary")),
        name="ssd_scan_bwd_finish" if finish else "ssd_scan_fwd",
    )(*args)


def _final_norm_kernel(x_ref, g_ref, o_ref):
    x = x_ref[...]
    ms = jnp.mean(x * x, axis=-1, keepdims=True)
    o_ref[...] = x * lax.rsqrt(ms + EPS) * g_ref[...]


def _final_norm(x, g, rows):
    d = x.shape[1]
    return pl.pallas_call(
        _final_norm_kernel,
        grid=(rows // ROW_TILE,),
        in_specs=[pl.BlockSpec((ROW_TILE, d), lambda i: (i, 0)), pl.BlockSpec((1, d), lambda i: (0, 0))],
        out_specs=pl.BlockSpec((ROW_TILE, d), lambda i: (i, 0)),
        out_shape=jax.ShapeDtypeStruct((rows, d), F32),
        compiler_params=_params(("arbitrary",)),
        name="final_norm",
    )(x, g)


def _rope_tables(seq, pattern):
    rows = seq // GRID_W
    pos_r = jnp.broadcast_to(jnp.arange(rows, dtype=F32)[:, None], (rows, GRID_W)).reshape(seq)
    pos_c = jnp.broadcast_to(jnp.arange(GRID_W, dtype=F32)[None, :], (rows, GRID_W)).reshape(seq)
    n = DA_DIM // 4
    inv = ROPE_BASE ** (-jnp.arange(n, dtype=F32) / n)
    ang = jnp.concatenate([pos_r[:, None] * inv, pos_c[:, None] * inv], axis=-1)
    cos, sin = jnp.cos(ang), jnp.sin(ang)
    one, zero = jnp.ones_like(cos), jnp.zeros_like(cos)
    cos_t = jnp.concatenate([{"a": cos, "b": cos, "i": one}[c] for c in pattern], axis=1)
    sin_t = jnp.concatenate([{"a": -sin, "b": sin, "i": zero}[c] for c in pattern], axis=1)
    pad_c = jnp.ones((ROW_TILE, cos_t.shape[1]), F32)
    pad_s = jnp.zeros((ROW_TILE, cos_t.shape[1]), F32)
    return jnp.concatenate([cos_t, pad_c], axis=0), jnp.concatenate([sin_t, pad_s], axis=0)


def kernel(x, c, ctx, c_ctx, mod_w, mod_b, norm_mix, norm_ffn, ffn_w1, ffn_w2, attn_w_in, mla_g_q, mla_w_uq, mla_g_kv, mla_w_ukv, da_lam_q1, da_lam_k1, da_lam_q2, da_lam_k2, da_g_sub, attn_w_out, ssd_w_in, ssd_conv_w, ssd_conv_b, ssd_dt_bias, ssd_a_log, ssd_d, ssd_g_norm, ssd_w_out, final_g):
    bsz, seq, d = x.shape
    ctx_len = ctx.shape[1]
    depth = mod_w.shape[0]
    assert seq % ROW_TILE == 0 and (bsz * ctx_len) % ROW_TILE == 0 and bsz + 1 <= MOD_ROWS
    assert seq % ATT_TILE == 0 and ctx_len % ATT_TILE == 0 and seq % GRID_W == 0
    lay = dict(batch=bsz, seq=seq, ctx=ctx_len, tiles_per_seq=seq // ROW_TILE, lat_tiles=bsz * seq // ROW_TILE)
    n_lat = bsz * seq
    lat_tiles = n_lat // ROW_TILE

    xs = jnp.concatenate([x.reshape(n_lat, d), ctx.reshape(bsz * ctx_len, d)], axis=0)
    cvec = jnp.zeros((MOD_ROWS, d), F32).at[:bsz].set(c).at[bsz].set(c_ctx)
    mods = _modulation(cvec, mod_w, mod_b)

    cos_da, sin_da = _rope_tables(seq, "abab")
    cos_mq, sin_mq = _rope_tables(seq, "iiiiabii")
    cos_kr, sin_kr = _rope_tables(seq, "abii")

    inner = ssd_w_out.shape[1]
    n_heads = inner // SSD_HEADDIM

    for l in range(depth):
        last = l == depth - 1
        i = l // 2
        mix_tiles = None
        out_tiles = lat_tiles if last else None
        pre_mix = dict(pre_gain=norm_mix[l][None, :], pre_mod=(mods, l * 6 + 0, l * 6 + 1))
        if l % 2 == 0:
            lambda_init = 0.8 - 0.6 * math.exp(-0.3 * l)
            w_in = attn_w_in[i]
            o = 0
            w_q = w_in[:, o:o + DA_COLS] * (DA_DIM ** -0.5 * LOG2E); o += DA_COLS
            w_k = w_in[:, o:o + DA_COLS]; o += DA_COLS
            w_v = w_in[:, o:o + DA_COLS]; o += DA_COLS
            w_c = w_in[:, o:o + MLA_Q_RANK + MLA_KV_RANK]; o += MLA_Q_RANK + MLA_KV_RANK
            w_kr = jnp.pad(w_in[:, o:o + MLA_ROPE], ((0, 0), (0, LANE - MLA_ROPE)))
            w_qk = jnp.concatenate([w_q, w_k], axis=1).astype(BF16)
            mla_scale = (MLA_NOPE + MLA_ROPE) ** -0.5 * LOG2E
            w_uq = jnp.pad((mla_w_uq[i] * mla_scale).reshape(MLA_Q_RANK, MLA_HEADS, MLA_NOPE + MLA_ROPE),
                           ((0, 0), (0, 0), (0, MLA_QK_PAD - MLA_NOPE - MLA_ROPE)))
            w_uq = w_uq.reshape(MLA_Q_RANK, MLA_HEADS * MLA_QK_PAD).astype(BF16)
            w_ukv = mla_w_ukv[i].reshape(MLA_KV_RANK, MLA_HEADS, MLA_NOPE + MLA_V)
            w_ukv = jnp.concatenate([w_ukv[:, :, :MLA_NOPE].reshape(MLA_KV_RANK, -1),
                                     w_ukv[:, :, MLA_NOPE:].reshape(MLA_KV_RANK, -1)], axis=1).astype(BF16)

            qk_a = _linear(xs, w_qk, tn=1024, out_dtype=BF16, lay=lay, name="attn_in_qk", m_tiles=mix_tiles,
                           epi="rope", epi_args=(cos_da, sin_da), **pre_mix)
            v_a = _linear(xs, w_v.astype(BF16), tn=1024, out_dtype=BF16, lay=lay, name="attn_in_v",
                          m_tiles=mix_tiles, **pre_mix)
            cq_ckv = _linear(xs, w_c.astype(BF16), tn=MLA_Q_RANK + MLA_KV_RANK, out_dtype=F32, lay=lay,
                             name="attn_in_c", m_tiles=mix_tiles, **pre_mix)
            k_r = _linear(xs, w_kr.astype(BF16), tn=LANE, out_dtype=BF16, lay=lay, name="attn_in_kr",
                          m_tiles=mix_tiles, epi="rope", epi_args=(cos_kr, sin_kr), **pre_mix)
            q_m = _linear(cq_ckv, w_uq, tn=MLA_HEADS * MLA_QK_PAD, out_dtype=BF16, lay=lay, name="mla_uq",
                          k=MLA_Q_RANK, x_col_block=0, pre_gain=mla_g_q[i][None, :],
                          epi="rope", epi_args=(cos_mq, sin_mq))
            kv_m = _linear(cq_ckv, w_ukv, tn=MLA_HEADS * (MLA_NOPE + MLA_V), out_dtype=BF16, lay=lay, name="mla_ukv",
                           k=MLA_KV_RANK, x_col_block=MLA_Q_RANK // MLA_KV_RANK, pre_gain=mla_g_kv[i][None, :])
            m_rows = xs.shape[0]
            k_m = jnp.concatenate(
                [kv_m[:, :MLA_HEADS * MLA_NOPE].reshape(m_rows, MLA_HEADS, MLA_NOPE),
                 jnp.broadcast_to(k_r[:, None, :], (m_rows, MLA_HEADS, LANE))], axis=-1,
            ).reshape(m_rows, MLA_HEADS * MLA_QK_PAD)
            lam_vecs = jnp.stack([da_lam_q1[i], da_lam_k1[i], da_lam_q2[i], da_lam_k2[i]]).astype(F32)
            o_att = _attention(qk_a, v_a, q_m, k_m, kv_m, lam_vecs, da_g_sub[i][None, :].astype(F32),
                               lay=lay, lambda_init=lambda_init)
            xs = _linear(o_att, attn_w_out[i].astype(BF16), tn=1024, out_dtype=F32, lay=lay, name="attn_out",
                         m_tiles=out_tiles, epi="gate_res", epi_args=(xs, mods, l * 6 + 2))
        else:
            w_in = ssd_w_in[i]
            n_zx = inner + inner + 2 * SSD_GROUPS * SSD_STATE
            zx = _linear(xs, w_in[:, :n_zx].astype(BF16), tn=1024, out_dtype=BF16, lay=lay, name="ssd_in_zx",
                         m_tiles=mix_tiles, **pre_mix)
            dt = _linear(xs, w_in[:, n_zx:].astype(BF16), tn=2 * n_heads, out_dtype=F32, lay=lay, name="ssd_in_dt",
                         m_tiles=mix_tiles, epi="softplus", epi_args=(ssd_dt_bias[i].reshape(1, -1).astype(F32),),
                         **pre_mix)
            xbc = _ssd_conv(zx, ssd_conv_w[i].astype(F32), ssd_conv_b[i][None, :].astype(F32), lay=lay, col0=inner)
            a_log = ssd_a_log[i].reshape(1, -1).astype(F32)
            y_f = _ssd_scan(xbc, dt, a_log, lay=lay, direction=0)
            dsk = jnp.repeat(ssd_d[i].astype(F32), SSD_HEADDIM)[None, :]
            y = _ssd_scan(xbc, dt, a_log, lay=lay, direction=1,
                          finish_args=(y_f, zx, dsk, ssd_g_norm[i][None, :].astype(F32)))
            xs = _linear(y, ssd_w_out[i].astype(BF16), tn=1024, out_dtype=F32, lay=lay, name="ssd_out",
                         m_tiles=out_tiles, epi="gate_res", epi_args=(xs, mods, l * 6 + 2))
        hid = _linear(xs, ffn_w1[l].astype(BF16), tn=1024, out_dtype=BF16, lay=lay, name="ffn_up", m_tiles=out_tiles,
                      pre_gain=norm_ffn[l][None, :], pre_mod=(mods, l * 6 + 3, l * 6 + 4), epi="relu2")
        xs = _linear(hid, ffn_w2[l].astype(BF16), tn=512, out_dtype=F32, lay=lay, name="ffn_down", m_tiles=out_tiles,
                     epi="gate_res", epi_args=(xs, mods, l * 6 + 5))

    return _final_norm(xs, final_g[None, :].astype(F32), n_lat).reshape(bsz, seq, d)
```

```python
import functools
import math

import jax
import jax.numpy as jnp
from jax import lax
from jax.experimental import pallas as pl
from jax.experimental.pallas import tpu as pltpu

F32 = jnp.float32
BF16 = jnp.bfloat16

EPS = 1e-6
ROPE_BASE = 10000.0
GRID_W = 64
LOG2E = 1.4426950408889634

DA_HEADS = 8
DA_DIM = 64
DA_COLS = DA_HEADS * 2 * DA_DIM
MLA_HEADS = 8
MLA_NOPE = 128
MLA_ROPE = 64
MLA_V = 128
MLA_Q_RANK = 512
MLA_KV_RANK = 256
MLA_QK_PAD = 256
SSD_HEADDIM = 64
SSD_GROUPS = 8
SSD_STATE = 128
SSD_CONV = 5
SSD_CHUNK = 128

LANE = 128
ROW_TILE = 1024
ATT_TQ = 256
ATT_TK = 1024
ATT_KC = 128
ATT_UNROLL = 4
CONV_ROWS = 256
HALO = 16
MOD_ROWS = 8
VMEM_LIMIT = 56 * 1024 * 1024


def _silu(x):
    return x * (1.0 / (1.0 + jnp.exp(-x)))


def _params(sem):
    return pltpu.CompilerParams(dimension_semantics=sem, vmem_limit_bytes=VMEM_LIMIT)


def _mod_kernel(c_ref, w_ref, b_ref, o_ref):
    s = _silu(c_ref[...]).astype(BF16)
    o_ref[0] = jnp.dot(s, w_ref[0].astype(BF16), preferred_element_type=F32) + b_ref[0]


def _modulation(cvec, mod_w, mod_b):
    depth, d, _ = mod_w.shape
    tn = min(1024, d)
    nj = d // tn
    return pl.pallas_call(
        _mod_kernel,
        grid=(depth, 6, nj),
        in_specs=[
            pl.BlockSpec((MOD_ROWS, d), lambda l, r, j: (0, 0)),
            pl.BlockSpec((1, d, tn), lambda l, r, j: (l, 0, r * nj + j)),
            pl.BlockSpec((1, 1, tn), lambda l, r, j: (l * 6 + r, 0, j)),
        ],
        out_specs=pl.BlockSpec((1, MOD_ROWS, tn), lambda l, r, j: (l * 6 + r, 0, j)),
        out_shape=jax.ShapeDtypeStruct((depth * 6, MOD_ROWS, d), F32),
        compiler_params=_params(("arbitrary", "arbitrary", "arbitrary")),
        name="modulation",
    )(cvec, mod_w, mod_b.reshape(depth * 6, 1, d))


def _linear_kernel(*refs, pre, pre_mod, epi, tiles_per_seq, n_batch, tn, period):
    it = iter(refs)
    x_ref, w_ref = next(it), next(it)
    g_ref = next(it) if pre else None
    sh_ref = next(it) if pre_mod else None
    sc_ref = next(it) if pre_mod else None
    if epi == "softplus":
        bias_ref = next(it)
    elif epi == "rope":
        cos_ref, sin_ref = next(it), next(it)
    elif epi == "gate_res":
        res_ref, gate_ref = next(it), next(it)
    o_ref = next(it)
    h_ref = next(it) if pre else None

    grp = jnp.minimum(pl.program_id(0) // tiles_per_seq, n_batch)

    if pre:
        @pl.when(pl.program_id(1) == 0)
        def _():
            x32 = x_ref[...].astype(F32)
            ms = jnp.mean(x32 * x32, axis=-1, keepdims=True)
            y = x32 * lax.rsqrt(ms + EPS) * g_ref[...]
            if pre_mod:
                y = y * (1.0 + sc_ref[0, pl.ds(grp, 1), :]) + sh_ref[0, pl.ds(grp, 1), :]
            h_ref[...] = y.astype(BF16)
        lhs = h_ref[...]
    else:
        lhs = x_ref[...]

    acc = jnp.dot(lhs, w_ref[...], preferred_element_type=F32)

    if epi == "none":
        o_ref[...] = acc.astype(o_ref.dtype)
    elif epi == "relu2":
        r = jnp.maximum(acc, 0.0)
        o_ref[...] = (r * r).astype(o_ref.dtype)
    elif epi == "softplus":
        v = acc + bias_ref[...]
        o_ref[...] = (jnp.maximum(v, 0.0) + jnp.log(1.0 + jnp.exp(-jnp.abs(v)))).astype(o_ref.dtype)
    elif epi == "rope":
        cos, sin = cos_ref[...], sin_ref[...]
        lane = lax.broadcasted_iota(jnp.int32, cos.shape, 1)
        first = (lane % (2 * 32)) < 32
        for s in range(tn // period):
            a = acc[:, s * period:(s + 1) * period]
            partner = jnp.where(first, pltpu.roll(a, period - 32, 1), pltpu.roll(a, 32, 1))
            o_ref[:, s * period:(s + 1) * period] = (a * cos + partner * sin).astype(o_ref.dtype)
    elif epi == "gate_res":
        o_ref[...] = res_ref[...] + gate_ref[0, pl.ds(grp, 1), :] * acc


def _linear(x, w, *, tn, out_dtype, lay, name, k=None, x_col_block=0, m_tiles=None,
            pre_gain=None, pre_mod=None, epi="none", epi_args=()):
    m = x.shape[0]
    k = x.shape[1] if k is None else k
    n = w.shape[1]
    tm = ROW_TILE
    mt = m // tm if m_tiles is None else m_tiles
    nt = n // tn
    assert n % tn == 0 and w.shape[0] == k
    pre = pre_gain is not None
    period = None

    args = [x, w]
    specs = [pl.BlockSpec((tm, k), lambda i, j: (i, x_col_block)),
             pl.BlockSpec((k, tn), lambda i, j: (0, j))]
    if pre:
        args.append(pre_gain)
        specs.append(pl.BlockSpec((1, k), lambda i, j: (0, 0)))
    if pre_mod is not None:
        mods, shift_idx, scale_idx = pre_mod
        args += [mods, mods]
        specs += [pl.BlockSpec((1, MOD_ROWS, k), lambda i, j: (shift_idx, 0, 0)),
                  pl.BlockSpec((1, MOD_ROWS, k), lambda i, j: (scale_idx, 0, 0))]
    if epi == "softplus":
        (bias,) = epi_args
        args.append(bias)
        specs.append(pl.BlockSpec((1, tn), lambda i, j: (0, j)))
    elif epi == "rope":
        cos, sin = epi_args
        period = cos.shape[1]
        n_lat, tps = lay["lat_tiles"], lay["tiles_per_seq"]
        tab_map = lambda i, j: (jnp.where(i < n_lat, i % tps, tps), 0)
        args += [cos, sin]
        specs += [pl.BlockSpec((tm, period), tab_map), pl.BlockSpec((tm, period), tab_map)]
    elif epi == "gate_res":
        res, mods, gate_idx = epi_args
        args += [res, mods]
        specs += [pl.BlockSpec((tm, tn), lambda i, j: (i, j)),
                  pl.BlockSpec((1, MOD_ROWS, tn), lambda i, j: (gate_idx, 0, j))]

    kern = functools.partial(_linear_kernel, pre=pre, pre_mod=pre_mod is not None, epi=epi,
                             tiles_per_seq=lay["tiles_per_seq"], n_batch=lay["batch"], tn=tn, period=period)
    return pl.pallas_call(
        kern,
        grid=(mt, nt),
        in_specs=specs,
        out_specs=pl.BlockSpec((tm, tn), lambda i, j: (i, j)),
        out_shape=jax.ShapeDtypeStruct((mt * tm, n), out_dtype),
        scratch_shapes=[pltpu.VMEM((tm, k), BF16)] if pre else [],
        compiler_params=_params(("arbitrary", "arbitrary")),
        name=name,
    )(*args)


def _attn_kernel(lam_ref, gsub_ref, qa_ref, qm_ref, kac_ref, vac_ref, kmc_ref, vmc_ref,
                 ka_ref, va_ref, km_ref, vm_ref, o_ref, qta_ref, qtm_ref, m_ref, l_ref, acc_ref,
                 *, lambda_init, n_steps, n_lat_q):
    qi, t = pl.program_id(0), pl.program_id(1)
    tq = qa_ref.shape[0]
    ctx_len, tk, kc = kac_ref.shape[0], ka_ref.shape[0], ATT_KC

    @pl.when(t == 0)
    def _init():
        m_ref[...] = jnp.full(m_ref.shape, -jnp.inf, F32)
        l_ref[...] = jnp.zeros(l_ref.shape, F32)
        acc_ref[...] = jnp.zeros(acc_ref.shape, F32)
        first = lax.broadcasted_iota(jnp.int32, (LANE, tq), 0) < DA_DIM
        for h in range(DA_HEADS):
            qt = qa_ref[:, h * LANE:(h + 1) * LANE].astype(F32).T
            qta_ref[2 * h] = jnp.where(first, qt, 0.0).astype(BF16)
            qta_ref[2 * h + 1] = jnp.where(first, 0.0, qt).astype(BF16)
        for h in range(MLA_HEADS):
            qtm_ref[h] = qm_ref[:, h * MLA_QK_PAD:(h + 1) * MLA_QK_PAD].astype(F32).T.astype(BF16)

    def update(idx, qt, k, v):
        s = jnp.dot(k, qt, preferred_element_type=F32)
        m_prev = m_ref[idx]
        m_next = jnp.maximum(m_prev, jnp.max(s, axis=0, keepdims=True))
        alpha = jnp.exp2(m_prev - m_next)
        p = jnp.exp2(s - m_next)
        l_ref[idx] = alpha * l_ref[idx] + jnp.sum(p, axis=0, keepdims=True)
        pv = lax.dot_general(v, p.astype(BF16), (((0,), (0,)), ((), ())), preferred_element_type=F32)
        acc_ref[idx] = alpha * acc_ref[idx] + pv
        m_ref[idx] = m_next

    def chunk(ka, va, km, vm, r0):
        rows = pl.ds(r0, kc)
        for h in range(DA_HEADS):
            k = ka[rows, h * LANE:(h + 1) * LANE]
            v = va[rows, h * LANE:(h + 1) * LANE]
            for c in range(2):
                update(2 * h + c, qta_ref[2 * h + c], k, v)
        for h in range(MLA_HEADS):
            update(2 * DA_HEADS + h, qtm_ref[h], km[rows, h * MLA_QK_PAD:(h + 1) * MLA_QK_PAD],
                   vm[rows, h * MLA_V:(h + 1) * MLA_V])

    @pl.when(t == 0)
    def _ctx():
        def body(r, carry):
            chunk(kac_ref, vac_ref, kmc_ref, vmc_ref, pl.multiple_of(r * kc, kc))
            return carry
        lax.fori_loop(0, ctx_len // kc, body, 0)

    @pl.when(qi < n_lat_q)
    def _lat():
        def body(r, carry):
            chunk(ka_ref, va_ref, km_ref, vm_ref, pl.multiple_of(r * kc, kc))
            return carry
        lax.fori_loop(0, tk // kc, body, 0, unroll=ATT_UNROLL)

    @pl.when(t == n_steps - 1)
    def _finish():
        lv = lam_ref[...]
        lam = (jnp.exp(jnp.sum(lv[0:1] * lv[1:2], axis=1, keepdims=True))
               - jnp.exp(jnp.sum(lv[2:3] * lv[3:4], axis=1, keepdims=True)) + lambda_init)
        for h in range(DA_HEADS):
            o = acc_ref[2 * h] / l_ref[2 * h] - lam * (acc_ref[2 * h + 1] / l_ref[2 * h + 1])
            ms = jnp.mean(o * o, axis=0, keepdims=True)
            o = (o * lax.rsqrt(ms + EPS)).T * gsub_ref[...] * (1.0 - lambda_init)
            o_ref[:, h * LANE:(h + 1) * LANE] = o.astype(o_ref.dtype)
        for h in range(MLA_HEADS):
            idx = 2 * DA_HEADS + h
            o_ref[:, DA_COLS + h * MLA_V:DA_COLS + (h + 1) * MLA_V] = (acc_ref[idx] / l_ref[idx]).T.astype(o_ref.dtype)


def _attention(qk_a, v_a, q_m, k_m, kv_m, lam_vecs, g_sub, *, lay, lambda_init):
    m = qk_a.shape[0]
    tq, tk = ATT_TQ, ATT_TK
    s_len, ctx_len, bsz = lay["seq"], lay["ctx"], lay["batch"]
    n_lat_q = bsz * s_len // tq
    n_ctx_q = (m - bsz * s_len) // tq
    q_per_seq, q_per_ctx = s_len // tq, ctx_len // tq
    n_steps = s_len // tk
    ctx_base = bsz * s_len // ctx_len

    def batch_of(qi):
        return jnp.where(qi < n_lat_q, qi // q_per_seq, jnp.minimum((qi - n_lat_q) // q_per_ctx, bsz - 1))

    ctx_row = lambda col: (lambda qi, t: (ctx_base + batch_of(qi), col))
    lat_row = lambda col: (lambda qi, t: (batch_of(qi) * n_steps + jnp.where(qi < n_lat_q, t, 0), col))

    n_state = 2 * DA_HEADS + MLA_HEADS
    kern = functools.partial(_attn_kernel, lambda_init=lambda_init, n_steps=n_steps, n_lat_q=n_lat_q)
    return pl.pallas_call(
        kern,
        grid=(n_lat_q + n_ctx_q, n_steps),
        in_specs=[
            pl.BlockSpec((4, DA_DIM), lambda qi, t: (0, 0)),
            pl.BlockSpec((1, 2 * DA_DIM), lambda qi, t: (0, 0)),
            pl.BlockSpec((tq, DA_COLS), lambda qi, t: (qi, 0)),
            pl.BlockSpec((tq, MLA_HEADS * MLA_QK_PAD), lambda qi, t: (qi, 0)),
            pl.BlockSpec((ctx_len, DA_COLS), ctx_row(1)),
            pl.BlockSpec((ctx_len, DA_COLS), ctx_row(0)),
            pl.BlockSpec((ctx_len, MLA_HEADS * MLA_QK_PAD), ctx_row(0)),
            pl.BlockSpec((ctx_len, MLA_HEADS * MLA_V), ctx_row(1)),
            pl.BlockSpec((tk, DA_COLS), lat_row(1)),
            pl.BlockSpec((tk, DA_COLS), lat_row(0)),
            pl.BlockSpec((tk, MLA_HEADS * MLA_QK_PAD), lat_row(0)),
            pl.BlockSpec((tk, MLA_HEADS * MLA_V), lat_row(1)),
        ],
        out_specs=pl.BlockSpec((tq, DA_COLS + MLA_HEADS * MLA_V), lambda qi, t: (qi, 0)),
        out_shape=jax.ShapeDtypeStruct((m, DA_COLS + MLA_HEADS * MLA_V), BF16),
        scratch_shapes=[
            pltpu.VMEM((2 * DA_HEADS, LANE, tq), BF16),
            pltpu.VMEM((MLA_HEADS, MLA_QK_PAD, tq), BF16),
            pltpu.VMEM((n_state, 1, tq), F32),
            pltpu.VMEM((n_state, 1, tq), F32),
            pltpu.VMEM((n_state, LANE, tq), F32),
        ],
        compiler_params=_params(("arbitrary", "arbitrary")),
        name="attention",
    )(lam_vecs, g_sub, qk_a, q_m, qk_a, v_a, k_m, kv_m, qk_a, v_a, k_m, kv_m)


def _conv_kernel(xm_ref, xp_ref, xn_ref, w_ref, b_ref, o_ref, scr_ref, *, tm, seq, ctx, n_lat_rows):
    row0 = pl.program_id(0) * tm
    in_lat = row0 < n_lat_rows
    local = jnp.where(in_lat, row0 % seq, (row0 - n_lat_rows) % ctx)
    seq_len = jnp.where(in_lat, seq, ctx)
    keep_prev = (local != 0).astype(F32)
    keep_next = (local + tm != seq_len).astype(F32)
    scr_ref[0:HALO, :] = xp_ref[...].astype(F32) * keep_prev
    scr_ref[HALO:HALO + tm, :] = xm_ref[...].astype(F32)
    scr_ref[HALO + tm:2 * HALO + tm, :] = xn_ref[...].astype(F32) * keep_next
    acc = jnp.broadcast_to(b_ref[...], (tm, b_ref.shape[1]))
    for kk in range(SSD_CONV):
        acc = acc + w_ref[kk:kk + 1, :] * scr_ref[pl.ds(HALO - SSD_CONV // 2 + kk, tm), :]
    o_ref[...] = _silu(acc).astype(o_ref.dtype)


def _ssd_conv(zx, conv_w, conv_b, *, lay, col0):
    m = zx.shape[0]
    c = conv_w.shape[1]
    tm, tc = CONV_ROWS, 1024
    hb = tm // HALO
    last = m // HALO - 1
    cb0 = col0 // tc
    kern = functools.partial(_conv_kernel, tm=tm, seq=lay["seq"], ctx=lay["ctx"], n_lat_rows=lay["batch"] * lay["seq"])
    return pl.pallas_call(
        kern,
        grid=(m // tm, c // tc),
        in_specs=[
            pl.BlockSpec((tm, tc), lambda i, j: (i, cb0 + j)),
            pl.BlockSpec((HALO, tc), lambda i, j: (jnp.maximum(i * hb - 1, 0), cb0 + j)),
            pl.BlockSpec((HALO, tc), lambda i, j: (jnp.minimum((i + 1) * hb, last), cb0 + j)),
            pl.BlockSpec((SSD_CONV, tc), lambda i, j: (0, j)),
            pl.BlockSpec((1, tc), lambda i, j: (0, j)),
        ],
        out_specs=pl.BlockSpec((tm, tc), lambda i, j: (i, j)),
        out_shape=jax.ShapeDtypeStruct((m, c), BF16),
        scratch_shapes=[pltpu.VMEM((tm + 2 * HALO, tc), F32)],
        compiler_params=_params(("arbitrary", "arbitrary")),
        name="ssd_conv",
    )(zx, zx, zx, conv_w, conv_b)


def _ssd_scan_kernel(*refs, direction, finish, n_heads, n_batch, pad_rows):
    it = iter(refs)
    alog_ref, x_ref, b_ref, c_ref, dt_ref = (next(it) for _ in range(5))
    if finish:
        yf_ref, z_ref, dsk_ref, gn_ref = (next(it) for _ in range(4))
    o_ref = next(it)
    state_ref, lhs_ref, rhs_ref, bw_ref = (next(it) for _ in range(4))
    y_ref = next(it) if finish else o_ref

    L, N = SSD_CHUNK, SSD_STATE
    assert L == N == LANE
    heads_per_group = n_heads // SSD_GROUPS
    pairs_per_group = heads_per_group // 2

    def main():
        @pl.when(pl.program_id(1) == 0)
        def _():
            state_ref[...] = jnp.zeros(state_ref.shape, F32)

        a_rate = -jnp.exp(alog_ref[...])
        dt = dt_ref[...]
        ri = lax.broadcasted_iota(jnp.int32, (L, L), 0)
        ci = lax.broadcasted_iota(jnp.int32, (L, L), 1)
        tri = (ri >= ci) if direction == 0 else (ri <= ci)
        dta = dt * a_rate
        hi = dta.astype(BF16)
        r1 = dta - hi.astype(F32)
        mid = r1.astype(BF16)
        lo = (r1 - mid.astype(F32)).astype(BF16)
        tri_b = tri.astype(BF16)
        a = (jnp.dot(tri_b, hi, preferred_element_type=F32) + jnp.dot(tri_b, mid, preferred_element_type=F32)
             + jnp.dot(tri_b, lo, preferred_element_type=F32))
        a = a * LOG2E
        a_t = a.T
        dt_t = dt.T
        la_t = a_t - jnp.log2(dt_t)
        last = L - 1 if direction == 0 else 0
        first = lax.broadcasted_iota(jnp.int32, (1, LANE), 1) < SSD_HEADDIM

        for g in range(SSD_GROUPS):
            bg = b_ref[:, g * N:(g + 1) * N]
            cg = c_ref[:, g * N:(g + 1) * N]
            cb = lax.dot_general(cg, bg, (((1,), (1,)), ((), ())), preferred_element_type=F32)
            cg32 = cg.astype(F32)
            bg_t = bg.astype(F32).T
            for pp in range(pairs_per_group):
                p = g * pairs_per_group + pp
                xp = x_ref[:, p * LANE:(p + 1) * LANE]
                h_t = state_ref[p]
                dec = []
                for kk in range(2):
                    c = direction * n_heads + 2 * p + kk
                    tot = a_t[c:c + 1, last:last + 1]
                    a_b = jnp.broadcast_to(a[:, c:c + 1], (L, L))
                    seg_dt = jnp.exp2(jnp.where(tri, a_b - la_t[c:c + 1, :], -jnp.inf))
                    lhs_ref[p, :, kk * L:(kk + 1) * L] = (cb * seg_dt).astype(BF16)
                    lhs_ref[p, :, (2 + kk) * L:(3 + kk) * L] = (cg32 * jnp.exp2(a_b)).astype(BF16)
                    bw_ref[p, :, kk * L:(kk + 1) * L] = (bg_t * (jnp.exp2(tot - a_t[c:c + 1, :]) * dt_t[c:c + 1, :])).astype(BF16)
                    dec.append(jnp.exp2(tot))
                zero_x = jnp.zeros_like(xp)
                h_b = h_t.astype(BF16)
                zero_h = jnp.zeros_like(h_b)
                rhs_ref[p, 0:L] = jnp.where(first, xp, zero_x)
                rhs_ref[p, L:2 * L] = jnp.where(first, zero_x, xp)
                rhs_ref[p, 2 * L:3 * L] = jnp.where(first, h_b, zero_h)
                rhs_ref[p, 3 * L:4 * L] = jnp.where(first, zero_h, h_b)
                y = jnp.dot(lhs_ref[p], rhs_ref[p], preferred_element_type=F32)
                y_ref[:, p * LANE:(p + 1) * LANE] = y.astype(y_ref.dtype)
                upd = jnp.dot(bw_ref[p], rhs_ref[p, 0:2 * L], preferred_element_type=F32)
                state_ref[p] = h_t * jnp.where(first, dec[0], dec[1]) + upd

        if finish:
            gw = n_heads * SSD_HEADDIM // SSD_GROUPS
            for g in range(SSD_GROUPS):
                sl = slice(g * gw, (g + 1) * gw)
                v = yf_ref[:, sl].astype(F32) + y_ref[:, sl] + dsk_ref[:, sl] * x_ref[:, sl].astype(F32)
                v = v * _silu(z_ref[:, sl].astype(F32))
                ms = jnp.mean(v * v, axis=1, keepdims=True)
                o_ref[:, sl] = (v * lax.rsqrt(ms + EPS) * gn_ref[:, sl]).astype(o_ref.dtype)

    if pad_rows:
        pl.when(pl.program_id(0) < n_batch)(main)

        @pl.when(pl.program_id(0) >= n_batch)
        def _():
            o_ref[...] = jnp.zeros(o_ref.shape, o_ref.dtype)
    else:
        main()


def _ssd_scan(xbc, dt, a_log, *, lay, direction, finish_args=None):
    m = xbc.shape[0]
    L = SSD_CHUNK
    gn = SSD_GROUPS * SSD_STATE
    inner = xbc.shape[1] - 2 * gn
    n_heads = inner // SSD_HEADDIM
    bsz = lay["batch"]
    ncl, ncc = lay["seq"] // L, lay["ctx"] // L
    ctx_base = bsz * ncl
    finish = finish_args is not None

    n_real = bsz * (ncl + ncc)
    n_pad = (m // L - n_real) if finish else 0

    def chunk(b, t):
        if direction == 0:
            real = jnp.where(t < ncc, ctx_base + b * ncc + t, b * ncl + (t - ncc))
        else:
            real = jnp.where(t < ncc, ctx_base + b * ncc + (ncc - 1 - t), b * ncl + (ncl - 1 - (t - ncc)))
        if n_pad:
            return jnp.where(b < bsz, real, n_real + jnp.minimum(t, n_pad - 1))
        return real

    row = lambda b, t: (chunk(b, t), 0)
    args = [a_log, xbc, xbc, xbc, dt]
    specs = [
        pl.BlockSpec((1, 2 * n_heads), lambda b, t: (0, 0)),
        pl.BlockSpec((L, inner), row),
        pl.BlockSpec((L, gn), lambda b, t: (chunk(b, t), inner // gn)),
        pl.BlockSpec((L, gn), lambda b, t: (chunk(b, t), inner // gn + 1)),
        pl.BlockSpec((L, 2 * n_heads), row),
    ]
    n_pairs = n_heads // 2
    scratch = [pltpu.VMEM((n_pairs, SSD_STATE, LANE), F32),
               pltpu.VMEM((n_pairs, L, 4 * L), BF16),
               pltpu.VMEM((n_pairs, 4 * L, LANE), BF16),
               pltpu.VMEM((n_pairs, SSD_STATE, 2 * L), BF16)]
    if finish:
        y_f, zx, dsk, g_norm = finish_args
        args += [y_f, zx, dsk, g_norm]
        specs += [pl.BlockSpec((L, inner), row), pl.BlockSpec((L, inner), row),
                  pl.BlockSpec((1, inner), lambda b, t: (0, 0)), pl.BlockSpec((1, inner), lambda b, t: (0, 0))]
        scratch.append(pltpu.VMEM((L, inner), F32))
    kern = functools.partial(_ssd_scan_kernel, direction=direction, finish=finish, n_heads=n_heads,
                             n_batch=bsz, pad_rows=n_pad > 0)
    return pl.pallas_call(
        kern,
        grid=(bsz + (1 if n_pad else 0), ncc + ncl),
        in_specs=specs,
        out_specs=pl.BlockSpec((L, inner), row),
        out_shape=jax.ShapeDtypeStruct((m, inner), BF16),
        scratch_shapes=scratch,
        compiler_params=_params(("arbitrary", "arbitrary")),
        name="ssd_scan_bwd_finish" if finish else "ssd_scan_fwd",
    )(*args)


def _final_norm_kernel(x_ref, g_ref, o_ref):
    x = x_ref[...]
    ms = jnp.mean(x * x, axis=-1, keepdims=True)
    o_ref[...] = x * lax.rsqrt(ms + EPS) * g_ref[...]


def _final_norm(x, g, rows):
    d = x.shape[1]
    return pl.pallas_call(
        _final_norm_kernel,
        grid=(rows // ROW_TILE,),
        in_specs=[pl.BlockSpec((ROW_TILE, d), lambda i: (i, 0)), pl.BlockSpec((1, d), lambda i: (0, 0))],
        out_specs=pl.BlockSpec((ROW_TILE, d), lambda i: (i, 0)),
        out_shape=jax.ShapeDtypeStruct((rows, d), F32),
        compiler_params=_params(("arbitrary",)),
        name="final_norm",
    )(x, g)


def _rope_tables(seq, pattern):
    rows = seq // GRID_W
    pos_r = jnp.broadcast_to(jnp.arange(rows, dtype=F32)[:, None], (rows, GRID_W)).reshape(seq)
    pos_c = jnp.broadcast_to(jnp.arange(GRID_W, dtype=F32)[None, :], (rows, GRID_W)).reshape(seq)
    n = DA_DIM // 4
    inv = ROPE_BASE ** (-jnp.arange(n, dtype=F32) / n)
    ang = jnp.concatenate([pos_r[:, None] * inv, pos_c[:, None] * inv], axis=-1)
    cos, sin = jnp.cos(ang), jnp.sin(ang)
    one, zero = jnp.ones_like(cos), jnp.zeros_like(cos)
    cos_t = jnp.concatenate([{"a": cos, "b": cos, "i": one}[c] for c in pattern], axis=1)
    sin_t = jnp.concatenate([{"a": -sin, "b": sin, "i": zero}[c] for c in pattern], axis=1)
    pad_c = jnp.ones((ROW_TILE, cos_t.shape[1]), F32)
    pad_s = jnp.zeros((ROW_TILE, cos_t.shape[1]), F32)
    return jnp.concatenate([cos_t, pad_c], axis=0), jnp.concatenate([sin_t, pad_s], axis=0)


def kernel(x, c, ctx, c_ctx, mod_w, mod_b, norm_mix, norm_ffn, ffn_w1, ffn_w2, attn_w_in, mla_g_q, mla_w_uq, mla_g_kv, mla_w_ukv, da_lam_q1, da_lam_k1, da_lam_q2, da_lam_k2, da_g_sub, attn_w_out, ssd_w_in, ssd_conv_w, ssd_conv_b, ssd_dt_bias, ssd_a_log, ssd_d, ssd_g_norm, ssd_w_out, final_g):
    bsz, seq, d = x.shape
    ctx_len = ctx.shape[1]
    depth = mod_w.shape[0]
    assert seq % ROW_TILE == 0 and bsz + 1 <= MOD_ROWS
    assert seq % ATT_TK == 0 and ctx_len % ATT_TQ == 0 and ctx_len % ATT_KC == 0 and seq % GRID_W == 0
    lay = dict(batch=bsz, seq=seq, ctx=ctx_len, tiles_per_seq=seq // ROW_TILE, lat_tiles=bsz * seq // ROW_TILE)
    n_lat = bsz * seq
    lat_tiles = n_lat // ROW_TILE

    n_pad = -(n_lat + bsz * ctx_len) % ROW_TILE
    xs = jnp.concatenate([x.reshape(n_lat, d), ctx.reshape(bsz * ctx_len, d), jnp.zeros((n_pad, d), F32)], axis=0)
    cvec = jnp.zeros((MOD_ROWS, d), F32).at[:bsz].set(c).at[bsz].set(c_ctx)
    mods = _modulation(cvec, mod_w, mod_b)

    cos_da, sin_da = _rope_tables(seq, "abab")
    cos_mq, sin_mq = _rope_tables(seq, "iiiiabii")
    cos_kr, sin_kr = _rope_tables(seq, "abii")

    inner = ssd_w_out.shape[1]
    n_heads = inner // SSD_HEADDIM

    for l in range(depth):
        last = l == depth - 1
        i = l // 2
        mix_tiles = None
        out_tiles = lat_tiles if last else None
        pre_mix = dict(pre_gain=norm_mix[l][None, :], pre_mod=(mods, l * 6 + 0, l * 6 + 1))
        if l % 2 == 0:
            lambda_init = 0.8 - 0.6 * math.exp(-0.3 * l)
            w_in = attn_w_in[i]
            o = 0
            w_q = w_in[:, o:o + DA_COLS] * (DA_DIM ** -0.5 * LOG2E); o += DA_COLS
            w_k = w_in[:, o:o + DA_COLS]; o += DA_COLS
            w_v = w_in[:, o:o + DA_COLS]; o += DA_COLS
            w_c = w_in[:, o:o + MLA_Q_RANK + MLA_KV_RANK]; o += MLA_Q_RANK + MLA_KV_RANK
            w_kr = jnp.pad(w_in[:, o:o + MLA_ROPE], ((0, 0), (0, LANE - MLA_ROPE)))
            w_qk = jnp.concatenate([w_q, w_k], axis=1).astype(BF16)
            mla_scale = (MLA_NOPE + MLA_ROPE) ** -0.5 * LOG2E
            w_uq = jnp.pad((mla_w_uq[i] * mla_scale).reshape(MLA_Q_RANK, MLA_HEADS, MLA_NOPE + MLA_ROPE),
                           ((0, 0), (0, 0), (0, MLA_QK_PAD - MLA_NOPE - MLA_ROPE)))
            w_uq = w_uq.reshape(MLA_Q_RANK, MLA_HEADS * MLA_QK_PAD).astype(BF16)
            w_ukv = mla_w_ukv[i].reshape(MLA_KV_RANK, MLA_HEADS, MLA_NOPE + MLA_V)
            w_ukv = jnp.concatenate([w_ukv[:, :, :MLA_NOPE].reshape(MLA_KV_RANK, -1),
                                     w_ukv[:, :, MLA_NOPE:].reshape(MLA_KV_RANK, -1)], axis=1).astype(BF16)

            qk_a = _linear(xs, w_qk, tn=1024, out_dtype=BF16, lay=lay, name="attn_in_qk", m_tiles=mix_tiles,
                           epi="rope", epi_args=(cos_da, sin_da), **pre_mix)
            v_a = _linear(xs, w_v.astype(BF16), tn=1024, out_dtype=BF16, lay=lay, name="attn_in_v",
                          m_tiles=mix_tiles, **pre_mix)
            cq_ckv = _linear(xs, w_c.astype(BF16), tn=MLA_Q_RANK + MLA_KV_RANK, out_dtype=F32, lay=lay,
                             name="attn_in_c", m_tiles=mix_tiles, **pre_mix)
            k_r = _linear(xs, w_kr.astype(BF16), tn=LANE, out_dtype=BF16, lay=lay, name="attn_in_kr",
                          m_tiles=mix_tiles, epi="rope", epi_args=(cos_kr, sin_kr), **pre_mix)
            q_m = _linear(cq_ckv, w_uq, tn=MLA_HEADS * MLA_QK_PAD, out_dtype=BF16, lay=lay, name="mla_uq",
                          k=MLA_Q_RANK, x_col_block=0, pre_gain=mla_g_q[i][None, :],
                          epi="rope", epi_args=(cos_mq, sin_mq))
            kv_m = _linear(cq_ckv, w_ukv, tn=MLA_HEADS * (MLA_NOPE + MLA_V), out_dtype=BF16, lay=lay, name="mla_ukv",
                           k=MLA_KV_RANK, x_col_block=MLA_Q_RANK // MLA_KV_RANK, pre_gain=mla_g_kv[i][None, :])
            m_rows = xs.shape[0]
            k_m = jnp.concatenate(
                [kv_m[:, :MLA_HEADS * MLA_NOPE].reshape(m_rows, MLA_HEADS, MLA_NOPE),
                 jnp.broadcast_to(k_r[:, None, :], (m_rows, MLA_HEADS, LANE))], axis=-1,
            ).reshape(m_rows, MLA_HEADS * MLA_QK_PAD)
            lam_vecs = jnp.stack([da_lam_q1[i], da_lam_k1[i], da_lam_q2[i], da_lam_k2[i]]).astype(F32)
            o_att = _attention(qk_a, v_a, q_m, k_m, kv_m, lam_vecs, da_g_sub[i][None, :].astype(F32),
                               lay=lay, lambda_init=lambda_init)
            xs = _linear(o_att, attn_w_out[i].astype(BF16), tn=1024, out_dtype=F32, lay=lay, name="attn_out",
                         m_tiles=out_tiles, epi="gate_res", epi_args=(xs, mods, l * 6 + 2))
        else:
            w_in = ssd_w_in[i]
            n_zx = inner + inner + 2 * SSD_GROUPS * SSD_STATE
            zx = _linear(xs, w_in[:, :n_zx].astype(BF16), tn=1024, out_dtype=BF16, lay=lay, name="ssd_in_zx",
                         m_tiles=mix_tiles, **pre_mix)
            dt = _linear(xs, w_in[:, n_zx:].astype(BF16), tn=2 * n_heads, out_dtype=F32, lay=lay, name="ssd_in_dt",
                         m_tiles=mix_tiles, epi="softplus", epi_args=(ssd_dt_bias[i].reshape(1, -1).astype(F32),),
                         **pre_mix)
            xbc = _ssd_conv(zx, ssd_conv_w[i].astype(F32), ssd_conv_b[i][None, :].astype(F32), lay=lay, col0=inner)
            a_log = ssd_a_log[i].reshape(1, -1).astype(F32)
            y_f = _ssd_scan(xbc, dt, a_log, lay=lay, direction=0)
            dsk = jnp.repeat(ssd_d[i].astype(F32), SSD_HEADDIM)[None, :]
            y = _ssd_scan(xbc, dt, a_log, lay=lay, direction=1,
                          finish_args=(y_f, zx, dsk, ssd_g_norm[i][None, :].astype(F32)))
            xs = _linear(y, ssd_w_out[i].astype(BF16), tn=1024, out_dtype=F32, lay=lay, name="ssd_out",
                         m_tiles=out_tiles, epi="gate_res", epi_args=(xs, mods, l * 6 + 2))
        hid = _linear(xs, ffn_w1[l].astype(BF16), tn=1024, out_dtype=BF16, lay=lay, name="ffn_up", m_tiles=out_tiles,
                      pre_gain=norm_ffn[l][None, :], pre_mod=(mods, l * 6 + 3, l * 6 + 4), epi="relu2")
        xs = _linear(hid, ffn_w2[l].astype(BF16), tn=256, out_dtype=F32, lay=lay, name="ffn_down", m_tiles=out_tiles,
                     epi="gate_res", epi_args=(xs, mods, l * 6 + 5))

    return _final_norm(xs, final_g[None, :].astype(F32), n_lat).reshape(bsz, seq, d)
```

```python
import functools
import math

import jax
import jax.numpy as jnp
from jax import lax
from jax.experimental import pallas as pl
from jax.experimental.pallas import tpu as pltpu

F32 = jnp.float32
BF16 = jnp.bfloat16

EPS = 1e-6
ROPE_BASE = 10000.0
GRID_W = 64
LOG2E = 1.4426950408889634

DA_HEADS = 8
DA_DIM = 64
DA_COLS = DA_HEADS * 2 * DA_DIM
MLA_HEADS = 8
MLA_NOPE = 128
MLA_ROPE = 64
MLA_V = 128
MLA_Q_RANK = 512
MLA_KV_RANK = 256
MLA_QK_PAD = 256
SSD_HEADDIM = 64
SSD_GROUPS = 8
SSD_STATE = 128
SSD_CONV = 5
SSD_CHUNK = 128

LANE = 128
ROW_TILE = 1024
ATT_TQ = 256
ATT_TK = 1024
ATT_KC = 128
ATT_UNROLL = 8
CONV_ROWS = 256
CONV_BLOCK = 128
CONV_LANES = 256
HALO = 16
MOD_ROWS = 8
VMEM_LIMIT = 56 * 1024 * 1024


def _silu(x):
    return x * (1.0 / (1.0 + jnp.exp(-x)))


def _params(sem):
    return pltpu.CompilerParams(dimension_semantics=sem, vmem_limit_bytes=VMEM_LIMIT)


def _mod_kernel(c_ref, w_ref, b_ref, o_ref):
    s = _silu(c_ref[...]).astype(BF16)
    o_ref[0] = jnp.dot(s, w_ref[0].astype(BF16), preferred_element_type=F32) + b_ref[0]


def _modulation(cvec, mod_w, mod_b):
    depth, d, _ = mod_w.shape
    tn = min(1024, d)
    nj = d // tn
    return pl.pallas_call(
        _mod_kernel,
        grid=(depth, 6, nj),
        in_specs=[
            pl.BlockSpec((MOD_ROWS, d), lambda l, r, j: (0, 0)),
            pl.BlockSpec((1, d, tn), lambda l, r, j: (l, 0, r * nj + j)),
            pl.BlockSpec((1, 1, tn), lambda l, r, j: (l * 6 + r, 0, j)),
        ],
        out_specs=pl.BlockSpec((1, MOD_ROWS, tn), lambda l, r, j: (l * 6 + r, 0, j)),
        out_shape=jax.ShapeDtypeStruct((depth * 6, MOD_ROWS, d), F32),
        compiler_params=_params(("arbitrary", "arbitrary", "arbitrary")),
        name="modulation",
    )(cvec, mod_w, mod_b.reshape(depth * 6, 1, d))


def _cast_kernel(w_ref, o_ref):
    o_ref[...] = w_ref[0].astype(o_ref.dtype)


def _cast_bf16(w, layer, col0=0, ncols=None):
    _, k, n = w.shape
    ncols = n - col0 if ncols is None else ncols
    tn = ncols if ncols <= 2048 else 1024
    tk = min(k, 1024)
    assert ncols % tn == 0 and col0 % tn == 0 and k % tk == 0
    cb0 = col0 // tn
    return pl.pallas_call(
        _cast_kernel,
        grid=(k // tk, ncols // tn),
        in_specs=[pl.BlockSpec((1, tk, tn), lambda i, j: (layer, i, cb0 + j))],
        out_specs=pl.BlockSpec((tk, tn), lambda i, j: (i, j)),
        out_shape=jax.ShapeDtypeStruct((k, ncols), BF16),
        compiler_params=_params(("arbitrary", "arbitrary")),
        name="cast_bf16",
    )(w)


def _linear_kernel(*refs, pre, pre_mod, epi, tiles_per_seq, n_batch, tn, period):
    it = iter(refs)
    x_ref, w_ref = next(it), next(it)
    g_ref = next(it) if pre else None
    sh_ref = next(it) if pre_mod else None
    sc_ref = next(it) if pre_mod else None
    if epi == "softplus":
        bias_ref = next(it)
    elif epi == "rope":
        cos_ref, sin_ref = next(it), next(it)
    elif epi == "gate_res":
        res_ref, gate_ref = next(it), next(it)
    elif epi == "mla_kv":
        kr_ref = next(it)
    o_ref = next(it)
    o2_ref = next(it) if epi == "mla_kv" else None
    h_ref = next(it) if pre else None

    grp = jnp.minimum(pl.program_id(0) // tiles_per_seq, n_batch)

    if pre:
        @pl.when(pl.program_id(1) == 0)
        def _():
            x32 = x_ref[...].astype(F32)
            ms = jnp.mean(x32 * x32, axis=-1, keepdims=True)
            y = x32 * lax.rsqrt(ms + EPS) * g_ref[...]
            if pre_mod:
                y = y * (1.0 + sc_ref[0, pl.ds(grp, 1), :]) + sh_ref[0, pl.ds(grp, 1), :]
            h_ref[...] = y.astype(BF16)
        lhs = h_ref[...]
    else:
        lhs = x_ref[...]

    acc = jnp.dot(lhs, w_ref[...], preferred_element_type=F32)

    if epi == "none":
        o_ref[...] = acc.astype(o_ref.dtype)
    elif epi == "relu2":
        r = jnp.maximum(acc, 0.0)
        o_ref[...] = (r * r).astype(o_ref.dtype)
    elif epi == "softplus":
        v = acc + bias_ref[...]
        o_ref[...] = (jnp.maximum(v, 0.0) + jnp.log(1.0 + jnp.exp(-jnp.abs(v)))).astype(o_ref.dtype)
    elif epi == "rope":
        cos, sin = cos_ref[...], sin_ref[...]
        lane = lax.broadcasted_iota(jnp.int32, cos.shape, 1)
        first = (lane % (2 * 32)) < 32
        for s in range(tn // period):
            a = acc[:, s * period:(s + 1) * period]
            partner = jnp.where(first, pltpu.roll(a, period - 32, 1), pltpu.roll(a, 32, 1))
            o_ref[:, s * period:(s + 1) * period] = (a * cos + partner * sin).astype(o_ref.dtype)
    elif epi == "gate_res":
        o_ref[...] = res_ref[...] + gate_ref[0, pl.ds(grp, 1), :] * acc
    elif epi == "mla_kv":
        for h in range(MLA_HEADS):
            o_ref[:, h * MLA_QK_PAD:h * MLA_QK_PAD + MLA_NOPE] = acc[:, h * MLA_NOPE:(h + 1) * MLA_NOPE].astype(o_ref.dtype)
            o_ref[:, h * MLA_QK_PAD + MLA_NOPE:(h + 1) * MLA_QK_PAD] = kr_ref[...]
        o2_ref[...] = acc[:, MLA_HEADS * MLA_NOPE:].astype(o2_ref.dtype)


def _linear(x, w, *, tn, out_dtype, lay, name, k=None, x_col_block=0, m_tiles=None,
            pre_gain=None, pre_mod=None, epi="none", epi_args=()):
    m = x.shape[0]
    k = x.shape[1] if k is None else k
    n = w.shape[1]
    tm = ROW_TILE
    mt = m // tm if m_tiles is None else m_tiles
    nt = n // tn
    assert n % tn == 0 and w.shape[0] == k
    pre = pre_gain is not None
    period = None

    args = [x, w]
    specs = [pl.BlockSpec((tm, k), lambda i, j: (i, x_col_block)),
             pl.BlockSpec((k, tn), lambda i, j: (0, j))]
    if pre:
        args.append(pre_gain)
        specs.append(pl.BlockSpec((1, k), lambda i, j: (0, 0)))
    if pre_mod is not None:
        mods, shift_idx, scale_idx = pre_mod
        args += [mods, mods]
        specs += [pl.BlockSpec((1, MOD_ROWS, k), lambda i, j: (shift_idx, 0, 0)),
                  pl.BlockSpec((1, MOD_ROWS, k), lambda i, j: (scale_idx, 0, 0))]
    if epi == "softplus":
        (bias,) = epi_args
        args.append(bias)
        specs.append(pl.BlockSpec((1, tn), lambda i, j: (0, j)))
    elif epi == "rope":
        cos, sin, per_col = epi_args
        period = cos.shape[1]
        n_lat, tps = lay["lat_tiles"], lay["tiles_per_seq"]
        tab_map = lambda i, j: (jnp.where(i < n_lat, i % tps, tps) + (j * (tps + 1) if per_col else 0), 0)
        args += [cos, sin]
        specs += [pl.BlockSpec((tm, period), tab_map), pl.BlockSpec((tm, period), tab_map)]
    elif epi == "gate_res":
        res, mods, gate_idx = epi_args
        args += [res, mods]
        specs += [pl.BlockSpec((tm, tn), lambda i, j: (i, j)),
                  pl.BlockSpec((1, MOD_ROWS, tn), lambda i, j: (gate_idx, 0, j))]
    out_specs = pl.BlockSpec((tm, tn), lambda i, j: (i, j))
    out_shape = jax.ShapeDtypeStruct((mt * tm, n), out_dtype)
    if epi == "mla_kv":
        (k_rope,) = epi_args
        assert nt == 1 and n == MLA_HEADS * (MLA_NOPE + MLA_V)
        args.append(k_rope)
        specs.append(pl.BlockSpec((tm, LANE), lambda i, j: (i, 0)))
        out_specs = [pl.BlockSpec((tm, MLA_HEADS * MLA_QK_PAD), lambda i, j: (i, 0)),
                     pl.BlockSpec((tm, MLA_HEADS * MLA_V), lambda i, j: (i, 0))]
        out_shape = [jax.ShapeDtypeStruct((mt * tm, MLA_HEADS * MLA_QK_PAD), out_dtype),
                     jax.ShapeDtypeStruct((mt * tm, MLA_HEADS * MLA_V), out_dtype)]

    kern = functools.partial(_linear_kernel, pre=pre, pre_mod=pre_mod is not None, epi=epi,
                             tiles_per_seq=lay["tiles_per_seq"], n_batch=lay["batch"], tn=tn, period=period)
    return pl.pallas_call(
        kern,
        grid=(mt, nt),
        in_specs=specs,
        out_specs=out_specs,
        out_shape=out_shape,
        scratch_shapes=[pltpu.VMEM((tm, k), BF16)] if pre else [],
        compiler_params=_params(("arbitrary", "arbitrary")),
        name=name,
    )(*args)


def _attn_kernel(lam_ref, gsub_ref, qa_ref, qm_ref, kac_ref, vac_ref, kmc_ref, vmc_ref,
                 ka_ref, va_ref, km_ref, vm_ref, o_ref, qta_ref, qtm_ref, m_ref, l_ref, acc_ref,
                 *, lambda_init, n_steps, n_lat_q):
    qi, t = pl.program_id(0), pl.program_id(1)
    tq = qa_ref.shape[0]
    ctx_len, tk, kc = kac_ref.shape[0], ka_ref.shape[0], ATT_KC

    @pl.when(t == 0)
    def _init():
        m_ref[...] = jnp.full(m_ref.shape, -jnp.inf, F32)
        l_ref[...] = jnp.zeros(l_ref.shape, F32)
        acc_ref[...] = jnp.zeros(acc_ref.shape, F32)
        first = lax.broadcasted_iota(jnp.int32, (LANE, tq), 0) < DA_DIM
        for h in range(DA_HEADS):
            qt = qa_ref[:, h * LANE:(h + 1) * LANE].astype(F32).T
            qta_ref[2 * h] = jnp.where(first, qt, 0.0).astype(BF16)
            qta_ref[2 * h + 1] = jnp.where(first, 0.0, qt).astype(BF16)
        for h in range(MLA_HEADS):
            qtm_ref[h] = qm_ref[:, h * MLA_QK_PAD:(h + 1) * MLA_QK_PAD].astype(F32).T.astype(BF16)

    def update(idx, qt, k, v):
        s = jnp.dot(k, qt, preferred_element_type=F32)
        m_prev = m_ref[idx]
        m_next = jnp.maximum(m_prev, jnp.max(s, axis=0, keepdims=True))
        alpha = jnp.exp2(m_prev - m_next)
        p = jnp.exp2(s - m_next)
        l_ref[idx] = alpha * l_ref[idx] + jnp.sum(p, axis=0, keepdims=True)
        pv = lax.dot_general(v, p.astype(BF16), (((0,), (0,)), ((), ())), preferred_element_type=F32)
        acc_ref[idx] = alpha * acc_ref[idx] + pv
        m_ref[idx] = m_next

    def chunk(ka, va, km, vm, r0):
        rows = pl.ds(r0, kc)
        for h in range(DA_HEADS):
            k = ka[rows, h * LANE:(h + 1) * LANE]
            v = va[rows, h * LANE:(h + 1) * LANE]
            for c in range(2):
                update(2 * h + c, qta_ref[2 * h + c], k, v)
        for h in range(MLA_HEADS):
            update(2 * DA_HEADS + h, qtm_ref[h], km[rows, h * MLA_QK_PAD:(h + 1) * MLA_QK_PAD],
                   vm[rows, h * MLA_V:(h + 1) * MLA_V])

    @pl.when(t == 0)
    def _ctx():
        def body(r, carry):
            chunk(kac_ref, vac_ref, kmc_ref, vmc_ref, pl.multiple_of(r * kc, kc))
            return carry
        lax.fori_loop(0, ctx_len // kc, body, 0)

    @pl.when(qi < n_lat_q)
    def _lat():
        def body(r, carry):
            chunk(ka_ref, va_ref, km_ref, vm_ref, pl.multiple_of(r * kc, kc))
            return carry
        lax.fori_loop(0, tk // kc, body, 0, unroll=ATT_UNROLL)

    @pl.when(t == n_steps - 1)
    def _finish():
        lv = lam_ref[...]
        lam = (jnp.exp(jnp.sum(lv[0:1] * lv[1:2], axis=1, keepdims=True))
               - jnp.exp(jnp.sum(lv[2:3] * lv[3:4], axis=1, keepdims=True)) + lambda_init)
        for h in range(DA_HEADS):
            o = acc_ref[2 * h] / l_ref[2 * h] - lam * (acc_ref[2 * h + 1] / l_ref[2 * h + 1])
            ms = jnp.mean(o * o, axis=0, keepdims=True)
            o = (o * lax.rsqrt(ms + EPS)).T * gsub_ref[...] * (1.0 - lambda_init)
            o_ref[:, h * LANE:(h + 1) * LANE] = o.astype(o_ref.dtype)
        for h in range(MLA_HEADS):
            idx = 2 * DA_HEADS + h
            o_ref[:, DA_COLS + h * MLA_V:DA_COLS + (h + 1) * MLA_V] = (acc_ref[idx] / l_ref[idx]).T.astype(o_ref.dtype)


def _attention(qk_a, v_a, q_m, k_m, v_m, lam_vecs, g_sub, *, lay, lambda_init):
    m = qk_a.shape[0]
    tq, tk = ATT_TQ, ATT_TK
    s_len, ctx_len, bsz = lay["seq"], lay["ctx"], lay["batch"]
    n_lat_q = bsz * s_len // tq
    n_ctx_q = (m - bsz * s_len) // tq
    q_per_seq, q_per_ctx = s_len // tq, ctx_len // tq
    n_steps = s_len // tk
    ctx_base = bsz * s_len // ctx_len

    def batch_of(qi):
        return jnp.where(qi < n_lat_q, qi // q_per_seq, jnp.minimum((qi - n_lat_q) // q_per_ctx, bsz - 1))

    ctx_row = lambda col: (lambda qi, t: (ctx_base + batch_of(qi), col))
    lat_row = lambda col: (lambda qi, t: (batch_of(qi) * n_steps + jnp.where(qi < n_lat_q, t, 0), col))

    n_state = 2 * DA_HEADS + MLA_HEADS
    kern = functools.partial(_attn_kernel, lambda_init=lambda_init, n_steps=n_steps, n_lat_q=n_lat_q)
    return pl.pallas_call(
        kern,
        grid=(n_lat_q + n_ctx_q, n_steps),
        in_specs=[
            pl.BlockSpec((4, DA_DIM), lambda qi, t: (0, 0)),
            pl.BlockSpec((1, 2 * DA_DIM), lambda qi, t: (0, 0)),
            pl.BlockSpec((tq, DA_COLS), lambda qi, t: (qi, 0)),
            pl.BlockSpec((tq, MLA_HEADS * MLA_QK_PAD), lambda qi, t: (qi, 0)),
            pl.BlockSpec((ctx_len, DA_COLS), ctx_row(1)),
            pl.BlockSpec((ctx_len, DA_COLS), ctx_row(0)),
            pl.BlockSpec((ctx_len, MLA_HEADS * MLA_QK_PAD), ctx_row(0)),
            pl.BlockSpec((ctx_len, MLA_HEADS * MLA_V), ctx_row(0)),
            pl.BlockSpec((tk, DA_COLS), lat_row(1)),
            pl.BlockSpec((tk, DA_COLS), lat_row(0)),
            pl.BlockSpec((tk, MLA_HEADS * MLA_QK_PAD), lat_row(0)),
            pl.BlockSpec((tk, MLA_HEADS * MLA_V), lat_row(0)),
        ],
        out_specs=pl.BlockSpec((tq, DA_COLS + MLA_HEADS * MLA_V), lambda qi, t: (qi, 0)),
        out_shape=jax.ShapeDtypeStruct((m, DA_COLS + MLA_HEADS * MLA_V), BF16),
        scratch_shapes=[
            pltpu.VMEM((2 * DA_HEADS, LANE, tq), BF16),
            pltpu.VMEM((MLA_HEADS, MLA_QK_PAD, tq), BF16),
            pltpu.VMEM((n_state, 1, tq), F32),
            pltpu.VMEM((n_state, 1, tq), F32),
            pltpu.VMEM((n_state, LANE, tq), F32),
        ],
        compiler_params=_params(("arbitrary", "arbitrary")),
        name="attention",
    )(lam_vecs, g_sub, qk_a, q_m, qk_a, v_a, k_m, v_m, qk_a, v_a, k_m, v_m)


def _conv_kernel(xm_ref, xp_ref, xn_ref, w_ref, b_ref, s_ref, o_ref, *, tm, seq, ctx, n_lat_rows):
    row0 = pl.program_id(0) * tm
    in_lat = row0 < n_lat_rows
    local = jnp.where(in_lat, row0 % seq, (row0 - n_lat_rows) % ctx)
    seq_len = jnp.where(in_lat, seq, ctx)
    keep_prev = (local != 0).astype(BF16)
    keep_next = (local + tm != seq_len).astype(BF16)
    blk = CONV_BLOCK
    for r in range(tm // blk):
        prev = xp_ref[...] * keep_prev if r == 0 else xm_ref[r * blk - HALO:r * blk, :]
        nxt = xn_ref[...] * keep_next if r == tm // blk - 1 else xm_ref[(r + 1) * blk:(r + 1) * blk + HALO, :]
        x_ext = jnp.concatenate([prev, xm_ref[r * blk:(r + 1) * blk, :], nxt], axis=0)
        for cs in range(0, x_ext.shape[1], CONV_LANES):
            cols = slice(cs, cs + CONV_LANES)
            acc = jnp.broadcast_to(b_ref[:, cols], (blk, CONV_LANES))
            for kk in range(SSD_CONV):
                acc = acc + w_ref[kk:kk + 1, cols] * jnp.dot(s_ref[kk], x_ext[:, cols], preferred_element_type=F32)
            o_ref[r * blk:(r + 1) * blk, cols] = _silu(acc).astype(o_ref.dtype)


def _ssd_conv(zx, conv_w, conv_b, *, lay, col0):
    m = zx.shape[0]
    c = conv_w.shape[1]
    tm, tc = CONV_ROWS, 1024
    hb = tm // HALO
    last = m // HALO - 1
    cb0 = col0 // tc
    rows = lax.broadcasted_iota(jnp.int32, (SSD_CONV, CONV_BLOCK, CONV_BLOCK + 2 * HALO), 1)
    cols = lax.broadcasted_iota(jnp.int32, (SSD_CONV, CONV_BLOCK, CONV_BLOCK + 2 * HALO), 2)
    taps = lax.broadcasted_iota(jnp.int32, (SSD_CONV, CONV_BLOCK, CONV_BLOCK + 2 * HALO), 0)
    shifts = (cols == rows + HALO - SSD_CONV // 2 + taps).astype(BF16)
    kern = functools.partial(_conv_kernel, tm=tm, seq=lay["seq"], ctx=lay["ctx"], n_lat_rows=lay["batch"] * lay["seq"])
    return pl.pallas_call(
        kern,
        grid=(m // tm, c // tc),
        in_specs=[
            pl.BlockSpec((tm, tc), lambda i, j: (i, cb0 + j)),
            pl.BlockSpec((HALO, tc), lambda i, j: (jnp.maximum(i * hb - 1, 0), cb0 + j)),
            pl.BlockSpec((HALO, tc), lambda i, j: (jnp.minimum((i + 1) * hb, last), cb0 + j)),
            pl.BlockSpec((SSD_CONV, tc), lambda i, j: (0, j)),
            pl.BlockSpec((1, tc), lambda i, j: (0, j)),
            pl.BlockSpec(shifts.shape, lambda i, j: (0, 0, 0)),
        ],
        out_specs=pl.BlockSpec((tm, tc), lambda i, j: (i, j)),
        out_shape=jax.ShapeDtypeStruct((m, c), BF16),
        compiler_params=_params(("arbitrary", "arbitrary")),
        name="ssd_conv",
    )(zx, zx, zx, conv_w, conv_b, shifts)


def _ssd_scan_kernel(*refs, direction, finish, n_heads, n_batch, pad_rows):
    it = iter(refs)
    alog_ref, x_ref, b_ref, c_ref, dt_ref = (next(it) for _ in range(5))
    if finish:
        yf_ref, z_ref, dsk_ref, gn_ref = (next(it) for _ in range(4))
    o_ref = next(it)
    state_ref, lhs_ref, rhs_ref, bw_ref = (next(it) for _ in range(4))
    y_ref = next(it) if finish else o_ref

    L, N = SSD_CHUNK, SSD_STATE
    assert L == N == LANE
    heads_per_group = n_heads // SSD_GROUPS
    pairs_per_group = heads_per_group // 2

    def main():
        @pl.when(pl.program_id(1) == 0)
        def _():
            state_ref[...] = jnp.zeros(state_ref.shape, F32)

        a_rate = -jnp.exp(alog_ref[...])
        dt = dt_ref[...]
        ri = lax.broadcasted_iota(jnp.int32, (L, L), 0)
        ci = lax.broadcasted_iota(jnp.int32, (L, L), 1)
        tri = (ri >= ci) if direction == 0 else (ri <= ci)
        dta = dt * a_rate
        hi = dta.astype(BF16)
        r1 = dta - hi.astype(F32)
        mid = r1.astype(BF16)
        lo = (r1 - mid.astype(F32)).astype(BF16)
        tri_b = tri.astype(BF16)
        a = (jnp.dot(tri_b, hi, preferred_element_type=F32) + jnp.dot(tri_b, mid, preferred_element_type=F32)
             + jnp.dot(tri_b, lo, preferred_element_type=F32))
        a = a * LOG2E
        a_t = a.T
        dt_t = dt.T
        la_t = a_t - jnp.log2(dt_t)
        last = L - 1 if direction == 0 else 0
        first = lax.broadcasted_iota(jnp.int32, (1, LANE), 1) < SSD_HEADDIM

        for g in range(SSD_GROUPS):
            bg = b_ref[:, g * N:(g + 1) * N]
            cg = c_ref[:, g * N:(g + 1) * N]
            cb = lax.dot_general(cg, bg, (((1,), (1,)), ((), ())), preferred_element_type=F32)
            cg32 = cg.astype(F32)
            bg_t = bg.astype(F32).T
            for pp in range(pairs_per_group):
                p = g * pairs_per_group + pp
                xp = x_ref[:, p * LANE:(p + 1) * LANE]
                h_t = state_ref[p]
                dec = []
                for kk in range(2):
                    c = direction * n_heads + 2 * p + kk
                    tot = a_t[c:c + 1, last:last + 1]
                    a_b = jnp.broadcast_to(a[:, c:c + 1], (L, L))
                    seg_dt = jnp.exp2(jnp.where(tri, a_b - la_t[c:c + 1, :], -jnp.inf))
                    lhs_ref[p, :, kk * L:(kk + 1) * L] = (cb * seg_dt).astype(BF16)
                    lhs_ref[p, :, (2 + kk) * L:(3 + kk) * L] = (cg32 * jnp.exp2(a_b)).astype(BF16)
                    bw_ref[p, :, kk * L:(kk + 1) * L] = (bg_t * (jnp.exp2(tot - a_t[c:c + 1, :]) * dt_t[c:c + 1, :])).astype(BF16)
                    dec.append(jnp.exp2(tot))
                zero_x = jnp.zeros_like(xp)
                h_b = h_t.astype(BF16)
                zero_h = jnp.zeros_like(h_b)
                rhs_ref[p, 0:L] = jnp.where(first, xp, zero_x)
                rhs_ref[p, L:2 * L] = jnp.where(first, zero_x, xp)
                rhs_ref[p, 2 * L:3 * L] = jnp.where(first, h_b, zero_h)
                rhs_ref[p, 3 * L:4 * L] = jnp.where(first, zero_h, h_b)
                y = jnp.dot(lhs_ref[p], rhs_ref[p], preferred_element_type=F32)
                y_ref[:, p * LANE:(p + 1) * LANE] = y.astype(y_ref.dtype)
                upd = jnp.dot(bw_ref[p], rhs_ref[p, 0:2 * L], preferred_element_type=F32)
                state_ref[p] = h_t * jnp.where(first, dec[0], dec[1]) + upd

        if finish:
            gw = n_heads * SSD_HEADDIM // SSD_GROUPS
            for g in range(SSD_GROUPS):
                sl = slice(g * gw, (g + 1) * gw)
                v = yf_ref[:, sl].astype(F32) + y_ref[:, sl] + dsk_ref[:, sl] * x_ref[:, sl].astype(F32)
                v = v * _silu(z_ref[:, sl].astype(F32))
                ms = jnp.mean(v * v, axis=1, keepdims=True)
                o_ref[:, sl] = (v * lax.rsqrt(ms + EPS) * gn_ref[:, sl]).astype(o_ref.dtype)

    if pad_rows:
        pl.when(pl.program_id(0) < n_batch)(main)

        @pl.when(pl.program_id(0) >= n_batch)
        def _():
            o_ref[...] = jnp.zeros(o_ref.shape, o_ref.dtype)
    else:
        main()


def _ssd_scan(xbc, dt, a_log, *, lay, direction, finish_args=None):
    m = xbc.shape[0]
    L = SSD_CHUNK
    gn = SSD_GROUPS * SSD_STATE
    inner = xbc.shape[1] - 2 * gn
    n_heads = inner // SSD_HEADDIM
    bsz = lay["batch"]
    ncl, ncc = lay["seq"] // L, lay["ctx"] // L
    ctx_base = bsz * ncl
    finish = finish_args is not None

    n_real = bsz * (ncl + ncc)
    n_pad = (m // L - n_real) if finish else 0

    def chunk(b, t):
        if direction == 0:
            real = jnp.where(t < ncc, ctx_base + b * ncc + t, b * ncl + (t - ncc))
        else:
            real = jnp.where(t < ncc, ctx_base + b * ncc + (ncc - 1 - t), b * ncl + (ncl - 1 - (t - ncc)))
        if n_pad:
            return jnp.where(b < bsz, real, n_real + jnp.minimum(t, n_pad - 1))
        return real

    row = lambda b, t: (chunk(b, t), 0)
    args = [a_log, xbc, xbc, xbc, dt]
    specs = [
        pl.BlockSpec((1, 2 * n_heads), lambda b, t: (0, 0)),
        pl.BlockSpec((L, inner), row),
        pl.BlockSpec((L, gn), lambda b, t: (chunk(b, t), inner // gn)),
        pl.BlockSpec((L, gn), lambda b, t: (chunk(b, t), inner // gn + 1)),
        pl.BlockSpec((L, 2 * n_heads), row),
    ]
    n_pairs = n_heads // 2
    scratch = [pltpu.VMEM((n_pairs, SSD_STATE, LANE), F32),
               pltpu.VMEM((n_pairs, L, 4 * L), BF16),
               pltpu.VMEM((n_pairs, 4 * L, LANE), BF16),
               pltpu.VMEM((n_pairs, SSD_STATE, 2 * L), BF16)]
    if finish:
        y_f, zx, dsk, g_norm = finish_args
        args += [y_f, zx, dsk, g_norm]
        specs += [pl.BlockSpec((L, inner), row), pl.BlockSpec((L, inner), row),
                  pl.BlockSpec((1, inner), lambda b, t: (0, 0)), pl.BlockSpec((1, inner), lambda b, t: (0, 0))]
        scratch.append(pltpu.VMEM((L, inner), F32))
    kern = functools.partial(_ssd_scan_kernel, direction=direction, finish=finish, n_heads=n_heads,
                             n_batch=bsz, pad_rows=n_pad > 0)
    return pl.pallas_call(
        kern,
        grid=(bsz + (1 if n_pad else 0), ncc + ncl),
        in_specs=specs,
        out_specs=pl.BlockSpec((L, inner), row),
        out_shape=jax.ShapeDtypeStruct((m, inner), BF16),
        scratch_shapes=scratch,
        compiler_params=_params(("arbitrary", "arbitrary")),
        name="ssd_scan_bwd_finish" if finish else "ssd_scan_fwd",
    )(*args)


def _final_norm_kernel(x_ref, g_ref, o_ref):
    x = x_ref[...]
    ms = jnp.mean(x * x, axis=-1, keepdims=True)
    o_ref[...] = x * lax.rsqrt(ms + EPS) * g_ref[...]


def _final_norm(x, g, rows):
    d = x.shape[1]
    return pl.pallas_call(
        _final_norm_kernel,
        grid=(rows // ROW_TILE,),
        in_specs=[pl.BlockSpec((ROW_TILE, d), lambda i: (i, 0)), pl.BlockSpec((1, d), lambda i: (0, 0))],
        out_specs=pl.BlockSpec((ROW_TILE, d), lambda i: (i, 0)),
        out_shape=jax.ShapeDtypeStruct((rows, d), F32),
        compiler_params=_params(("arbitrary",)),
        name="final_norm",
    )(x, g)


def _rope_tables(seq, pattern, scale=1.0):
    rows = seq // GRID_W
    pos_r = jnp.broadcast_to(jnp.arange(rows, dtype=F32)[:, None], (rows, GRID_W)).reshape(seq)
    pos_c = jnp.broadcast_to(jnp.arange(GRID_W, dtype=F32)[None, :], (rows, GRID_W)).reshape(seq)
    n = DA_DIM // 4
    inv = ROPE_BASE ** (-jnp.arange(n, dtype=F32) / n)
    ang = jnp.concatenate([pos_r[:, None] * inv, pos_c[:, None] * inv], axis=-1)
    cos, sin = jnp.cos(ang), jnp.sin(ang)
    one, zero = jnp.ones_like(cos), jnp.zeros_like(cos)
    cos_t = jnp.concatenate([{"a": cos, "b": cos, "i": one}[c] for c in pattern], axis=1)
    sin_t = jnp.concatenate([{"a": -sin, "b": sin, "i": zero}[c] for c in pattern], axis=1)
    pad_c = jnp.ones((ROW_TILE, cos_t.shape[1]), F32)
    pad_s = jnp.zeros((ROW_TILE, cos_t.shape[1]), F32)
    return jnp.concatenate([cos_t, pad_c], axis=0) * scale, jnp.concatenate([sin_t, pad_s], axis=0) * scale


def kernel(x, c, ctx, c_ctx, mod_w, mod_b, norm_mix, norm_ffn, ffn_w1, ffn_w2, attn_w_in, mla_g_q, mla_w_uq, mla_g_kv, mla_w_ukv, da_lam_q1, da_lam_k1, da_lam_q2, da_lam_k2, da_g_sub, attn_w_out, ssd_w_in, ssd_conv_w, ssd_conv_b, ssd_dt_bias, ssd_a_log, ssd_d, ssd_g_norm, ssd_w_out, final_g):
    bsz, seq, d = x.shape
    ctx_len = ctx.shape[1]
    depth = mod_w.shape[0]
    assert seq % ROW_TILE == 0 and bsz + 1 <= MOD_ROWS
    assert seq % ATT_TK == 0 and ctx_len % ATT_TQ == 0 and ctx_len % ATT_KC == 0 and seq % GRID_W == 0
    lay = dict(batch=bsz, seq=seq, ctx=ctx_len, tiles_per_seq=seq // ROW_TILE, lat_tiles=bsz * seq // ROW_TILE)
    n_lat = bsz * seq
    lat_tiles = n_lat // ROW_TILE

    n_pad = -(n_lat + bsz * ctx_len) % ROW_TILE
    xs = jnp.concatenate([x.reshape(n_lat, d), ctx.reshape(bsz * ctx_len, d), jnp.zeros((n_pad, d), F32)], axis=0)
    cvec = jnp.zeros((MOD_ROWS, d), F32).at[:bsz].set(c).at[bsz].set(c_ctx)
    mods = _modulation(cvec, mod_w, mod_b)

    cq, sq = _rope_tables(seq, "abab", DA_DIM ** -0.5 * LOG2E)
    ck, sk = _rope_tables(seq, "abab")
    cos_da, sin_da = jnp.concatenate([cq, ck], axis=0), jnp.concatenate([sq, sk], axis=0)
    cos_mq, sin_mq = _rope_tables(seq, "iiiiabii")
    cos_kr, sin_kr = _rope_tables(seq, "abii")

    inner = ssd_w_out.shape[1]
    n_heads = inner // SSD_HEADDIM

    for l in range(depth):
        last = l == depth - 1
        i = l // 2
        mix_tiles = None
        out_tiles = lat_tiles if last else None
        pre_mix = dict(pre_gain=norm_mix[l][None, :], pre_mod=(mods, l * 6 + 0, l * 6 + 1))
        if l % 2 == 0:
            lambda_init = 0.8 - 0.6 * math.exp(-0.3 * l)
            w_in = attn_w_in[i]
            c0 = 3 * DA_COLS
            w_qk = _cast_bf16(attn_w_in, i, 0, 2 * DA_COLS)
            w_v = _cast_bf16(attn_w_in, i, 2 * DA_COLS, DA_COLS)
            w_c = _cast_bf16(attn_w_in, i, c0, MLA_Q_RANK + MLA_KV_RANK)
            w_kr = jnp.pad(w_in[:, c0 + MLA_Q_RANK + MLA_KV_RANK:], ((0, 0), (0, LANE - MLA_ROPE))).astype(BF16)
            mla_scale = (MLA_NOPE + MLA_ROPE) ** -0.5 * LOG2E
            w_uq = jnp.pad((mla_w_uq[i] * mla_scale).reshape(MLA_Q_RANK, MLA_HEADS, MLA_NOPE + MLA_ROPE),
                           ((0, 0), (0, 0), (0, MLA_QK_PAD - MLA_NOPE - MLA_ROPE)))
            w_uq = w_uq.reshape(MLA_Q_RANK, MLA_HEADS * MLA_QK_PAD).astype(BF16)
            w_ukv = mla_w_ukv[i].reshape(MLA_KV_RANK, MLA_HEADS, MLA_NOPE + MLA_V)
            w_ukv = jnp.concatenate([w_ukv[:, :, :MLA_NOPE].reshape(MLA_KV_RANK, -1),
                                     w_ukv[:, :, MLA_NOPE:].reshape(MLA_KV_RANK, -1)], axis=1).astype(BF16)

            qk_a = _linear(xs, w_qk, tn=1024, out_dtype=BF16, lay=lay, name="attn_in_qk", m_tiles=mix_tiles,
                           epi="rope", epi_args=(cos_da, sin_da, True), **pre_mix)
            v_a = _linear(xs, w_v, tn=1024, out_dtype=BF16, lay=lay, name="attn_in_v",
                          m_tiles=mix_tiles, **pre_mix)
            cq_ckv = _linear(xs, w_c, tn=MLA_Q_RANK + MLA_KV_RANK, out_dtype=F32, lay=lay,
                             name="attn_in_c", m_tiles=mix_tiles, **pre_mix)
            k_r = _linear(xs, w_kr, tn=LANE, out_dtype=BF16, lay=lay, name="attn_in_kr",
                          m_tiles=mix_tiles, epi="rope", epi_args=(cos_kr, sin_kr, False), **pre_mix)
            q_m = _linear(cq_ckv, w_uq, tn=MLA_HEADS * MLA_QK_PAD, out_dtype=BF16, lay=lay, name="mla_uq",
                          k=MLA_Q_RANK, x_col_block=0, pre_gain=mla_g_q[i][None, :],
                          epi="rope", epi_args=(cos_mq, sin_mq, False))
            k_m, v_m = _linear(cq_ckv, w_ukv, tn=MLA_HEADS * (MLA_NOPE + MLA_V), out_dtype=BF16, lay=lay, name="mla_ukv",
                               k=MLA_KV_RANK, x_col_block=MLA_Q_RANK // MLA_KV_RANK, pre_gain=mla_g_kv[i][None, :],
                               epi="mla_kv", epi_args=(k_r,))
            lam_vecs = jnp.stack([da_lam_q1[i], da_lam_k1[i], da_lam_q2[i], da_lam_k2[i]]).astype(F32)
            o_att = _attention(qk_a, v_a, q_m, k_m, v_m, lam_vecs, da_g_sub[i][None, :].astype(F32),
                               lay=lay, lambda_init=lambda_init)
            xs = _linear(o_att, _cast_bf16(attn_w_out, i), tn=1024, out_dtype=F32, lay=lay, name="attn_out",
                         m_tiles=out_tiles, epi="gate_res", epi_args=(xs, mods, l * 6 + 2))
        else:
            n_zx = inner + inner + 2 * SSD_GROUPS * SSD_STATE
            zx = _linear(xs, _cast_bf16(ssd_w_in, i, 0, n_zx), tn=1024, out_dtype=BF16, lay=lay, name="ssd_in_zx",
                         m_tiles=mix_tiles, **pre_mix)
            dt = _linear(xs, _cast_bf16(ssd_w_in, i, n_zx, 2 * n_heads), tn=2 * n_heads, out_dtype=F32, lay=lay, name="ssd_in_dt",
                         m_tiles=mix_tiles, epi="softplus", epi_args=(ssd_dt_bias[i].reshape(1, -1).astype(F32),),
                         **pre_mix)
            xbc = _ssd_conv(zx, ssd_conv_w[i].astype(F32), ssd_conv_b[i][None, :].astype(F32), lay=lay, col0=inner)
            a_log = ssd_a_log[i].reshape(1, -1).astype(F32)
            y_f = _ssd_scan(xbc, dt, a_log, lay=lay, direction=0)
            dsk = jnp.repeat(ssd_d[i].astype(F32), SSD_HEADDIM)[None, :]
            y = _ssd_scan(xbc, dt, a_log, lay=lay, direction=1,
                          finish_args=(y_f, zx, dsk, ssd_g_norm[i][None, :].astype(F32)))
            xs = _linear(y, _cast_bf16(ssd_w_out, i), tn=1024, out_dtype=F32, lay=lay, name="ssd_out",
                         m_tiles=out_tiles, epi="gate_res", epi_args=(xs, mods, l * 6 + 2))
        hid = _linear(xs, _cast_bf16(ffn_w1, l), tn=1024, out_dtype=BF16, lay=lay, name="ffn_up", m_tiles=out_tiles,
                      pre_gain=norm_ffn[l][None, :], pre_mod=(mods, l * 6 + 3, l * 6 + 4), epi="relu2")
        xs = _linear(hid, _cast_bf16(ffn_w2, l), tn=256, out_dtype=F32, lay=lay, name="ffn_down", m_tiles=out_tiles,
                     epi="gate_res", epi_args=(xs, mods, l * 6 + 5))

    return _final_norm(xs, final_g[None, :].astype(F32), n_lat).reshape(bsz, seq, d)
```

```python
import functools
import math

import jax
import jax.numpy as jnp
from jax import lax
from jax.experimental import pallas as pl
from jax.experimental.pallas import tpu as pltpu

F32 = jnp.float32
BF16 = jnp.bfloat16

EPS = 1e-6
ROPE_BASE = 10000.0
GRID_W = 64
LOG2E = 1.4426950408889634

DA_HEADS = 8
DA_DIM = 64
DA_COLS = DA_HEADS * 2 * DA_DIM
MLA_HEADS = 8
MLA_NOPE = 128
MLA_ROPE = 64
MLA_V = 128
MLA_Q_RANK = 512
MLA_KV_RANK = 256
MLA_QK_PAD = 256
SSD_HEADDIM = 64
SSD_GROUPS = 8
SSD_STATE = 128
SSD_CONV = 5
SSD_CHUNK = 128

LANE = 128
ROW_TILE = 1024
ATT_TQ = 256
ATT_TK = 1024
ATT_KC = 128
ATT_UNROLL = 8
CONV_ROWS = 256
CONV_BLOCK = 128
CONV_LANES = 256
HALO = 16
MOD_ROWS = 8
VMEM_LIMIT = 56 * 1024 * 1024


def _silu(x):
    return x * (1.0 / (1.0 + jnp.exp(-x)))


def _params(sem):
    return pltpu.CompilerParams(dimension_semantics=sem, vmem_limit_bytes=VMEM_LIMIT)


def _mod_kernel(c_ref, w_ref, b_ref, o_ref):
    s = _silu(c_ref[...]).astype(BF16)
    o_ref[0] = jnp.dot(s, w_ref[0].astype(BF16), preferred_element_type=F32) + b_ref[0]


def _modulation(cvec, mod_w, mod_b):
    depth, d, _ = mod_w.shape
    tn = min(1024, d)
    nj = d // tn
    return pl.pallas_call(
        _mod_kernel,
        grid=(depth, 6, nj),
        in_specs=[
            pl.BlockSpec((MOD_ROWS, d), lambda l, r, j: (0, 0)),
            pl.BlockSpec((1, d, tn), lambda l, r, j: (l, 0, r * nj + j)),
            pl.BlockSpec((1, 1, tn), lambda l, r, j: (l * 6 + r, 0, j)),
        ],
        out_specs=pl.BlockSpec((1, MOD_ROWS, tn), lambda l, r, j: (l * 6 + r, 0, j)),
        out_shape=jax.ShapeDtypeStruct((depth * 6, MOD_ROWS, d), F32),
        compiler_params=_params(("arbitrary", "arbitrary", "arbitrary")),
        name="modulation",
    )(cvec, mod_w, mod_b.reshape(depth * 6, 1, d))


def _cast_kernel(w_ref, o_ref):
    o_ref[...] = w_ref[0].astype(o_ref.dtype)


def _cast_bf16(w, layer, col0=0, ncols=None):
    _, k, n = w.shape
    ncols = n - col0 if ncols is None else ncols
    tn = ncols if ncols <= 2048 else 1024
    tk = min(k, 1024)
    assert ncols % tn == 0 and col0 % tn == 0 and k % tk == 0
    cb0 = col0 // tn
    return pl.pallas_call(
        _cast_kernel,
        grid=(k // tk, ncols // tn),
        in_specs=[pl.BlockSpec((1, tk, tn), lambda i, j: (layer, i, cb0 + j))],
        out_specs=pl.BlockSpec((tk, tn), lambda i, j: (i, j)),
        out_shape=jax.ShapeDtypeStruct((k, ncols), BF16),
        compiler_params=_params(("arbitrary", "arbitrary")),
        name="cast_bf16",
    )(w)


def _linear_kernel(*refs, pre, pre_mod, epi, tiles_per_seq, n_batch, tn, period, rope_cols):
    it = iter(refs)
    x_ref, w_ref = next(it), next(it)
    g_ref = next(it) if pre else None
    sh_ref = next(it) if pre_mod else None
    sc_ref = next(it) if pre_mod else None
    if epi == "softplus":
        bias_ref = next(it)
    elif epi == "rope":
        cos_ref, sin_ref = next(it), next(it)
    elif epi == "gate_res":
        res_ref, gate_ref = next(it), next(it)
    elif epi == "mla_kv":
        kr_ref = next(it)
    o_ref = next(it)
    o2_ref = next(it) if epi == "mla_kv" else None
    h_ref = next(it) if pre else None

    grp = jnp.minimum(pl.program_id(0) // tiles_per_seq, n_batch)

    if pre:
        @pl.when(pl.program_id(1) == 0)
        def _():
            x32 = x_ref[...].astype(F32)
            ms = jnp.mean(x32 * x32, axis=-1, keepdims=True)
            y = x32 * lax.rsqrt(ms + EPS) * g_ref[...]
            if pre_mod:
                y = y * (1.0 + sc_ref[0, pl.ds(grp, 1), :]) + sh_ref[0, pl.ds(grp, 1), :]
            h_ref[...] = y.astype(BF16)
        lhs = h_ref[...]
    else:
        lhs = x_ref[...]

    acc = jnp.dot(lhs, w_ref[...], preferred_element_type=F32)

    if epi == "none":
        o_ref[...] = acc.astype(o_ref.dtype)
    elif epi == "relu2":
        r = jnp.maximum(acc, 0.0)
        o_ref[...] = (r * r).astype(o_ref.dtype)
    elif epi == "softplus":
        v = acc + bias_ref[...]
        o_ref[...] = (jnp.maximum(v, 0.0) + jnp.log(1.0 + jnp.exp(-jnp.abs(v)))).astype(o_ref.dtype)
    elif epi == "rope":
        cos, sin = cos_ref[...], sin_ref[...]
        lane = lax.broadcasted_iota(jnp.int32, cos.shape, 1)
        first = (lane % (2 * 32)) < 32
        for s in range(tn // period):
            a = acc[:, s * period:(s + 1) * period]
            if rope_cols is None or (rope_cols[0] <= s * period < rope_cols[1]
                                     and (s * period - rope_cols[0]) % rope_cols[2] == 0):
                partner = jnp.where(first, pltpu.roll(a, period - 32, 1), pltpu.roll(a, 32, 1))
                a = a * cos + partner * sin
            o_ref[:, s * period:(s + 1) * period] = a.astype(o_ref.dtype)
    elif epi == "gate_res":
        o_ref[...] = res_ref[...] + gate_ref[0, pl.ds(grp, 1), :] * acc
    elif epi == "mla_kv":
        for h in range(MLA_HEADS):
            o_ref[:, h * MLA_QK_PAD:h * MLA_QK_PAD + MLA_NOPE] = acc[:, h * MLA_NOPE:(h + 1) * MLA_NOPE].astype(o_ref.dtype)
            o_ref[:, h * MLA_QK_PAD + MLA_NOPE:(h + 1) * MLA_QK_PAD] = kr_ref[...].astype(o_ref.dtype)
        o2_ref[...] = acc[:, MLA_HEADS * MLA_NOPE:].astype(o2_ref.dtype)


def _linear(x, w, *, tn, out_dtype, lay, name, k=None, x_col_block=0, m_tiles=None,
            pre_gain=None, pre_mod=None, epi="none", epi_args=()):
    m = x.shape[0]
    k = x.shape[1] if k is None else k
    n = w.shape[1]
    tm = ROW_TILE
    mt = m // tm if m_tiles is None else m_tiles
    nt = n // tn
    assert n % tn == 0 and w.shape[0] == k
    pre = pre_gain is not None
    period = rope_cols = None

    args = [x, w]
    specs = [pl.BlockSpec((tm, k), lambda i, j: (i, x_col_block)),
             pl.BlockSpec((k, tn), lambda i, j: (0, j))]
    if pre:
        args.append(pre_gain)
        specs.append(pl.BlockSpec((1, k), lambda i, j: (0, 0)))
    if pre_mod is not None:
        mods, shift_idx, scale_idx = pre_mod
        args += [mods, mods]
        specs += [pl.BlockSpec((1, MOD_ROWS, k), lambda i, j: (shift_idx, 0, 0)),
                  pl.BlockSpec((1, MOD_ROWS, k), lambda i, j: (scale_idx, 0, 0))]
    if epi == "softplus":
        (bias,) = epi_args
        args.append(bias)
        specs.append(pl.BlockSpec((1, tn), lambda i, j: (0, j)))
    elif epi == "rope":
        cos, sin, per_col, rope_cols = epi_args
        period = cos.shape[1]
        n_lat, tps = lay["lat_tiles"], lay["tiles_per_seq"]
        tab_map = lambda i, j: (jnp.where(i < n_lat, i % tps, tps) + (j * (tps + 1) if per_col else 0), 0)
        args += [cos, sin]
        specs += [pl.BlockSpec((tm, period), tab_map), pl.BlockSpec((tm, period), tab_map)]
    elif epi == "gate_res":
        res, mods, gate_idx = epi_args
        args += [res, mods]
        specs += [pl.BlockSpec((tm, tn), lambda i, j: (i, j)),
                  pl.BlockSpec((1, MOD_ROWS, tn), lambda i, j: (gate_idx, 0, j))]
    out_specs = pl.BlockSpec((tm, tn), lambda i, j: (i, j))
    out_shape = jax.ShapeDtypeStruct((mt * tm, n), out_dtype)
    if epi == "mla_kv":
        k_rope, kr_col = epi_args
        assert nt == 1 and n == MLA_HEADS * (MLA_NOPE + MLA_V)
        args.append(k_rope)
        specs.append(pl.BlockSpec((tm, LANE), lambda i, j: (i, kr_col)))
        out_specs = [pl.BlockSpec((tm, MLA_HEADS * MLA_QK_PAD), lambda i, j: (i, 0)),
                     pl.BlockSpec((tm, MLA_HEADS * MLA_V), lambda i, j: (i, 0))]
        out_shape = [jax.ShapeDtypeStruct((mt * tm, MLA_HEADS * MLA_QK_PAD), out_dtype),
                     jax.ShapeDtypeStruct((mt * tm, MLA_HEADS * MLA_V), out_dtype)]

    kern = functools.partial(_linear_kernel, pre=pre, pre_mod=pre_mod is not None, epi=epi,
                             tiles_per_seq=lay["tiles_per_seq"], n_batch=lay["batch"], tn=tn, period=period,
                             rope_cols=rope_cols)
    return pl.pallas_call(
        kern,
        grid=(mt, nt),
        in_specs=specs,
        out_specs=out_specs,
        out_shape=out_shape,
        scratch_shapes=[pltpu.VMEM((tm, k), BF16)] if pre else [],
        compiler_params=_params(("arbitrary", "arbitrary")),
        name=name,
    )(*args)


def _attn_kernel(lam_ref, gsub_ref, qa_ref, qm_ref, kac_ref, vac_ref, kmc_ref, vmc_ref,
                 ka_ref, va_ref, km_ref, vm_ref, o_ref, qta_ref, qtm_ref, m_ref, l_ref, acc_ref,
                 *, lambda_init, n_steps, n_lat_q):
    qi, t = pl.program_id(0), pl.program_id(1)
    tq = qa_ref.shape[0]
    ctx_len, tk, kc = kac_ref.shape[0], ka_ref.shape[0], ATT_KC

    @pl.when(t == 0)
    def _init():
        m_ref[...] = jnp.full(m_ref.shape, -jnp.inf, F32)
        l_ref[...] = jnp.zeros(l_ref.shape, F32)
        acc_ref[...] = jnp.zeros(acc_ref.shape, F32)
        first = lax.broadcasted_iota(jnp.int32, (LANE, tq), 0) < DA_DIM
        for h in range(DA_HEADS):
            qt = qa_ref[:, h * LANE:(h + 1) * LANE].astype(F32).T
            qta_ref[2 * h] = jnp.where(first, qt, 0.0).astype(BF16)
            qta_ref[2 * h + 1] = jnp.where(first, 0.0, qt).astype(BF16)
        for h in range(MLA_HEADS):
            qtm_ref[h] = qm_ref[:, h * MLA_QK_PAD:(h + 1) * MLA_QK_PAD].astype(F32).T.astype(BF16)

    def update(idx, qt, k, v):
        s = jnp.dot(k, qt, preferred_element_type=F32)
        m_prev = m_ref[idx]
        m_next = jnp.maximum(m_prev, jnp.max(s, axis=0, keepdims=True))
        alpha = jnp.exp2(m_prev - m_next)
        p = jnp.exp2(s - m_next)
        l_ref[idx] = alpha * l_ref[idx] + jnp.sum(p, axis=0, keepdims=True)
        pv = lax.dot_general(v, p.astype(BF16), (((0,), (0,)), ((), ())), preferred_element_type=F32)
        acc_ref[idx] = alpha * acc_ref[idx] + pv
        m_ref[idx] = m_next

    def chunk(ka, va, km, vm, r0):
        rows = pl.ds(r0, kc)
        for h in range(DA_HEADS):
            k = ka[rows, h * LANE:(h + 1) * LANE]
            v = va[rows, h * LANE:(h + 1) * LANE]
            for c in range(2):
                update(2 * h + c, qta_ref[2 * h + c], k, v)
        for h in range(MLA_HEADS):
            update(2 * DA_HEADS + h, qtm_ref[h], km[rows, h * MLA_QK_PAD:(h + 1) * MLA_QK_PAD],
                   vm[rows, h * MLA_V:(h + 1) * MLA_V])

    @pl.when(t == 0)
    def _ctx():
        def body(r, carry):
            chunk(kac_ref, vac_ref, kmc_ref, vmc_ref, pl.multiple_of(r * kc, kc))
            return carry
        lax.fori_loop(0, ctx_len // kc, body, 0)

    @pl.when(qi < n_lat_q)
    def _lat():
        def body(r, carry):
            chunk(ka_ref, va_ref, km_ref, vm_ref, pl.multiple_of(r * kc, kc))
            return carry
        lax.fori_loop(0, tk // kc, body, 0, unroll=ATT_UNROLL)

    @pl.when(t == n_steps - 1)
    def _finish():
        lv = lam_ref[...]
        lam = (jnp.exp(jnp.sum(lv[0:1] * lv[1:2], axis=1, keepdims=True))
               - jnp.exp(jnp.sum(lv[2:3] * lv[3:4], axis=1, keepdims=True)) + lambda_init)
        for h in range(DA_HEADS):
            o = acc_ref[2 * h] / l_ref[2 * h] - lam * (acc_ref[2 * h + 1] / l_ref[2 * h + 1])
            ms = jnp.mean(o * o, axis=0, keepdims=True)
            o = (o * lax.rsqrt(ms + EPS)).T * gsub_ref[...] * (1.0 - lambda_init)
            o_ref[:, h * LANE:(h + 1) * LANE] = o.astype(o_ref.dtype)
        for h in range(MLA_HEADS):
            idx = 2 * DA_HEADS + h
            o_ref[:, DA_COLS + h * MLA_V:DA_COLS + (h + 1) * MLA_V] = (acc_ref[idx] / l_ref[idx]).T.astype(o_ref.dtype)


def _attention(qkv_a, q_m, k_m, v_m, lam_vecs, g_sub, *, lay, lambda_init):
    m = qkv_a.shape[0]
    tq, tk = ATT_TQ, ATT_TK
    s_len, ctx_len, bsz = lay["seq"], lay["ctx"], lay["batch"]
    n_lat_q = bsz * s_len // tq
    n_ctx_q = (m - bsz * s_len) // tq
    q_per_seq, q_per_ctx = s_len // tq, ctx_len // tq
    n_steps = s_len // tk
    ctx_base = bsz * s_len // ctx_len

    def batch_of(qi):
        return jnp.where(qi < n_lat_q, qi // q_per_seq, jnp.minimum((qi - n_lat_q) // q_per_ctx, bsz - 1))

    ctx_row = lambda col: (lambda qi, t: (ctx_base + batch_of(qi), col))
    lat_row = lambda col: (lambda qi, t: (batch_of(qi) * n_steps + jnp.where(qi < n_lat_q, t, 0), col))

    n_state = 2 * DA_HEADS + MLA_HEADS
    kern = functools.partial(_attn_kernel, lambda_init=lambda_init, n_steps=n_steps, n_lat_q=n_lat_q)
    return pl.pallas_call(
        kern,
        grid=(n_lat_q + n_ctx_q, n_steps),
        in_specs=[
            pl.BlockSpec((4, DA_DIM), lambda qi, t: (0, 0)),
            pl.BlockSpec((1, 2 * DA_DIM), lambda qi, t: (0, 0)),
            pl.BlockSpec((tq, DA_COLS), lambda qi, t: (qi, 0)),
            pl.BlockSpec((tq, MLA_HEADS * MLA_QK_PAD), lambda qi, t: (qi, 0)),
            pl.BlockSpec((ctx_len, DA_COLS), ctx_row(1)),
            pl.BlockSpec((ctx_len, DA_COLS), ctx_row(2)),
            pl.BlockSpec((ctx_len, MLA_HEADS * MLA_QK_PAD), ctx_row(0)),
            pl.BlockSpec((ctx_len, MLA_HEADS * MLA_V), ctx_row(0)),
            pl.BlockSpec((tk, DA_COLS), lat_row(1)),
            pl.BlockSpec((tk, DA_COLS), lat_row(2)),
            pl.BlockSpec((tk, MLA_HEADS * MLA_QK_PAD), lat_row(0)),
            pl.BlockSpec((tk, MLA_HEADS * MLA_V), lat_row(0)),
        ],
        out_specs=pl.BlockSpec((tq, DA_COLS + MLA_HEADS * MLA_V), lambda qi, t: (qi, 0)),
        out_shape=jax.ShapeDtypeStruct((m, DA_COLS + MLA_HEADS * MLA_V), BF16),
        scratch_shapes=[
            pltpu.VMEM((2 * DA_HEADS, LANE, tq), BF16),
            pltpu.VMEM((MLA_HEADS, MLA_QK_PAD, tq), BF16),
            pltpu.VMEM((n_state, 1, tq), F32),
            pltpu.VMEM((n_state, 1, tq), F32),
            pltpu.VMEM((n_state, LANE, tq), F32),
        ],
        compiler_params=_params(("arbitrary", "arbitrary")),
        name="attention",
    )(lam_vecs, g_sub, qkv_a, q_m, qkv_a, qkv_a, k_m, v_m, qkv_a, qkv_a, k_m, v_m)


def _conv_kernel(xm_ref, xp_ref, xn_ref, w_ref, b_ref, s_ref, o_ref, *, tm, seq, ctx, n_lat_rows):
    row0 = pl.program_id(0) * tm
    in_lat = row0 < n_lat_rows
    local = jnp.where(in_lat, row0 % seq, (row0 - n_lat_rows) % ctx)
    seq_len = jnp.where(in_lat, seq, ctx)
    keep_prev = (local != 0).astype(BF16)
    keep_next = (local + tm != seq_len).astype(BF16)
    blk = CONV_BLOCK
    for r in range(tm // blk):
        prev = xp_ref[...] * keep_prev if r == 0 else xm_ref[r * blk - HALO:r * blk, :]
        nxt = xn_ref[...] * keep_next if r == tm // blk - 1 else xm_ref[(r + 1) * blk:(r + 1) * blk + HALO, :]
        x_ext = jnp.concatenate([prev, xm_ref[r * blk:(r + 1) * blk, :], nxt], axis=0)
        for cs in range(0, x_ext.shape[1], CONV_LANES):
            cols = slice(cs, cs + CONV_LANES)
            acc = jnp.broadcast_to(b_ref[:, cols], (blk, CONV_LANES))
            for kk in range(SSD_CONV):
                acc = acc + w_ref[kk:kk + 1, cols] * jnp.dot(s_ref[kk], x_ext[:, cols], preferred_element_type=F32)
            o_ref[r * blk:(r + 1) * blk, cols] = _silu(acc).astype(o_ref.dtype)


def _ssd_conv(zx, conv_w, conv_b, *, lay, col0):
    m = zx.shape[0]
    c = conv_w.shape[1]
    tm, tc = CONV_ROWS, 1024
    hb = tm // HALO
    last = m // HALO - 1
    cb0 = col0 // tc
    rows = lax.broadcasted_iota(jnp.int32, (SSD_CONV, CONV_BLOCK, CONV_BLOCK + 2 * HALO), 1)
    cols = lax.broadcasted_iota(jnp.int32, (SSD_CONV, CONV_BLOCK, CONV_BLOCK + 2 * HALO), 2)
    taps = lax.broadcasted_iota(jnp.int32, (SSD_CONV, CONV_BLOCK, CONV_BLOCK + 2 * HALO), 0)
    shifts = (cols == rows + HALO - SSD_CONV // 2 + taps).astype(BF16)
    kern = functools.partial(_conv_kernel, tm=tm, seq=lay["seq"], ctx=lay["ctx"], n_lat_rows=lay["batch"] * lay["seq"])
    return pl.pallas_call(
        kern,
        grid=(m // tm, c // tc),
        in_specs=[
            pl.BlockSpec((tm, tc), lambda i, j: (i, cb0 + j)),
            pl.BlockSpec((HALO, tc), lambda i, j: (jnp.maximum(i * hb - 1, 0), cb0 + j)),
            pl.BlockSpec((HALO, tc), lambda i, j: (jnp.minimum((i + 1) * hb, last), cb0 + j)),
            pl.BlockSpec((SSD_CONV, tc), lambda i, j: (0, j)),
            pl.BlockSpec((1, tc), lambda i, j: (0, j)),
            pl.BlockSpec(shifts.shape, lambda i, j: (0, 0, 0)),
        ],
        out_specs=pl.BlockSpec((tm, tc), lambda i, j: (i, j)),
        out_shape=jax.ShapeDtypeStruct((m, c), BF16),
        compiler_params=_params(("arbitrary", "arbitrary")),
        name="ssd_conv",
    )(zx, zx, zx, conv_w, conv_b, shifts)


def _ssd_scan_kernel(*refs, direction, finish, n_heads, n_batch, pad_rows):
    it = iter(refs)
    alog_ref, x_ref, b_ref, c_ref, dt_ref = (next(it) for _ in range(5))
    if finish:
        yf_ref, z_ref, dsk_ref, gn_ref = (next(it) for _ in range(4))
    o_ref = next(it)
    state_ref, lhs_ref, rhs_ref, bw_ref = (next(it) for _ in range(4))
    y_ref = next(it) if finish else o_ref

    L, N = SSD_CHUNK, SSD_STATE
    assert L == N == LANE
    heads_per_group = n_heads // SSD_GROUPS
    pairs_per_group = heads_per_group // 2

    def main():
        @pl.when(pl.program_id(1) == 0)
        def _():
            state_ref[...] = jnp.zeros(state_ref.shape, F32)

        a_rate = -jnp.exp(alog_ref[...])
        dt = dt_ref[...]
        ri = lax.broadcasted_iota(jnp.int32, (L, L), 0)
        ci = lax.broadcasted_iota(jnp.int32, (L, L), 1)
        tri = (ri >= ci) if direction == 0 else (ri <= ci)
        dta = dt * a_rate
        hi = dta.astype(BF16)
        r1 = dta - hi.astype(F32)
        mid = r1.astype(BF16)
        lo = (r1 - mid.astype(F32)).astype(BF16)
        tri_b = tri.astype(BF16)
        a = (jnp.dot(tri_b, hi, preferred_element_type=F32) + jnp.dot(tri_b, mid, preferred_element_type=F32)
             + jnp.dot(tri_b, lo, preferred_element_type=F32))
        a = a * LOG2E
        a_t = a.T
        dt_t = dt.T
        la_t = a_t - jnp.log2(dt_t)
        last = L - 1 if direction == 0 else 0
        first = lax.broadcasted_iota(jnp.int32, (1, LANE), 1) < SSD_HEADDIM

        for g in range(SSD_GROUPS):
            bg = b_ref[:, g * N:(g + 1) * N]
            cg = c_ref[:, g * N:(g + 1) * N]
            cb = lax.dot_general(cg, bg, (((1,), (1,)), ((), ())), preferred_element_type=F32)
            cg32 = cg.astype(F32)
            bg_t = bg.astype(F32).T
            for pp in range(pairs_per_group):
                p = g * pairs_per_group + pp
                xp = x_ref[:, p * LANE:(p + 1) * LANE]
                h_t = state_ref[p]
                dec = []
                for kk in range(2):
                    c = direction * n_heads + 2 * p + kk
                    tot = a_t[c:c + 1, last:last + 1]
                    a_b = jnp.broadcast_to(a[:, c:c + 1], (L, L))
                    seg_dt = jnp.exp2(jnp.where(tri, a_b - la_t[c:c + 1, :], -jnp.inf))
                    lhs_ref[p, :, kk * L:(kk + 1) * L] = (cb * seg_dt).astype(BF16)
                    lhs_ref[p, :, (2 + kk) * L:(3 + kk) * L] = (cg32 * jnp.exp2(a_b)).astype(BF16)
                    bw_ref[p, :, kk * L:(kk + 1) * L] = (bg_t * (jnp.exp2(tot - a_t[c:c + 1, :]) * dt_t[c:c + 1, :])).astype(BF16)
                    dec.append(jnp.exp2(tot))
                zero_x = jnp.zeros_like(xp)
                h_b = h_t.astype(BF16)
                zero_h = jnp.zeros_like(h_b)
                rhs_ref[p, 0:L] = jnp.where(first, xp, zero_x)
                rhs_ref[p, L:2 * L] = jnp.where(first, zero_x, xp)
                rhs_ref[p, 2 * L:3 * L] = jnp.where(first, h_b, zero_h)
                rhs_ref[p, 3 * L:4 * L] = jnp.where(first, zero_h, h_b)
                y = jnp.dot(lhs_ref[p], rhs_ref[p], preferred_element_type=F32)
                y_ref[:, p * LANE:(p + 1) * LANE] = y.astype(y_ref.dtype)
                upd = jnp.dot(bw_ref[p], rhs_ref[p, 0:2 * L], preferred_element_type=F32)
                state_ref[p] = h_t * jnp.where(first, dec[0], dec[1]) + upd

        if finish:
            gw = n_heads * SSD_HEADDIM // SSD_GROUPS
            for g in range(SSD_GROUPS):
                sl = slice(g * gw, (g + 1) * gw)
                v = yf_ref[:, sl].astype(F32) + y_ref[:, sl] + dsk_ref[:, sl] * x_ref[:, sl].astype(F32)
                v = v * _silu(z_ref[:, sl].astype(F32))
                ms = jnp.mean(v * v, axis=1, keepdims=True)
                o_ref[:, sl] = (v * lax.rsqrt(ms + EPS) * gn_ref[:, sl]).astype(o_ref.dtype)

    if pad_rows:
        pl.when(pl.program_id(0) < n_batch)(main)

        @pl.when(pl.program_id(0) >= n_batch)
        def _():
            o_ref[...] = jnp.zeros(o_ref.shape, o_ref.dtype)
    else:
        main()


def _ssd_scan(xbc, dt, a_log, *, lay, direction, finish_args=None):
    m = xbc.shape[0]
    L = SSD_CHUNK
    gn = SSD_GROUPS * SSD_STATE
    inner = xbc.shape[1] - 2 * gn
    n_heads = inner // SSD_HEADDIM
    bsz = lay["batch"]
    ncl, ncc = lay["seq"] // L, lay["ctx"] // L
    ctx_base = bsz * ncl
    finish = finish_args is not None

    n_real = bsz * (ncl + ncc)
    n_pad = (m // L - n_real) if finish else 0

    def chunk(b, t):
        if direction == 0:
            real = jnp.where(t < ncc, ctx_base + b * ncc + t, b * ncl + (t - ncc))
        else:
            real = jnp.where(t < ncc, ctx_base + b * ncc + (ncc - 1 - t), b * ncl + (ncl - 1 - (t - ncc)))
        if n_pad:
            return jnp.where(b < bsz, real, n_real + jnp.minimum(t, n_pad - 1))
        return real

    row = lambda b, t: (chunk(b, t), 0)
    args = [a_log, xbc, xbc, xbc, dt]
    specs = [
        pl.BlockSpec((1, 2 * n_heads), lambda b, t: (0, 0)),
        pl.BlockSpec((L, inner), row),
        pl.BlockSpec((L, gn), lambda b, t: (chunk(b, t), inner // gn)),
        pl.BlockSpec((L, gn), lambda b, t: (chunk(b, t), inner // gn + 1)),
        pl.BlockSpec((L, 2 * n_heads), row),
    ]
    n_pairs = n_heads // 2
    scratch = [pltpu.VMEM((n_pairs, SSD_STATE, LANE), F32),
               pltpu.VMEM((n_pairs, L, 4 * L), BF16),
               pltpu.VMEM((n_pairs, 4 * L, LANE), BF16),
               pltpu.VMEM((n_pairs, SSD_STATE, 2 * L), BF16)]
    if finish:
        y_f, zx, dsk, g_norm = finish_args
        args += [y_f, zx, dsk, g_norm]
        specs += [pl.BlockSpec((L, inner), row), pl.BlockSpec((L, inner), row),
                  pl.BlockSpec((1, inner), lambda b, t: (0, 0)), pl.BlockSpec((1, inner), lambda b, t: (0, 0))]
        scratch.append(pltpu.VMEM((L, inner), F32))
    kern = functools.partial(_ssd_scan_kernel, direction=direction, finish=finish, n_heads=n_heads,
                             n_batch=bsz, pad_rows=n_pad > 0)
    return pl.pallas_call(
        kern,
        grid=(bsz + (1 if n_pad else 0), ncc + ncl),
        in_specs=specs,
        out_specs=pl.BlockSpec((L, inner), row),
        out_shape=jax.ShapeDtypeStruct((m, inner), BF16),
        scratch_shapes=scratch,
        compiler_params=_params(("arbitrary", "arbitrary")),
        name="ssd_scan_bwd_finish" if finish else "ssd_scan_fwd",
    )(*args)


def _final_norm_kernel(x_ref, g_ref, o_ref):
    x = x_ref[...]
    ms = jnp.mean(x * x, axis=-1, keepdims=True)
    o_ref[...] = x * lax.rsqrt(ms + EPS) * g_ref[...]


def _final_norm(x, g, rows):
    d = x.shape[1]
    return pl.pallas_call(
        _final_norm_kernel,
        grid=(rows // ROW_TILE,),
        in_specs=[pl.BlockSpec((ROW_TILE, d), lambda i: (i, 0)), pl.BlockSpec((1, d), lambda i: (0, 0))],
        out_specs=pl.BlockSpec((ROW_TILE, d), lambda i: (i, 0)),
        out_shape=jax.ShapeDtypeStruct((rows, d), F32),
        compiler_params=_params(("arbitrary",)),
        name="final_norm",
    )(x, g)


def _rope_tables(seq, pattern, scale=1.0):
    rows = seq // GRID_W
    pos_r = jnp.broadcast_to(jnp.arange(rows, dtype=F32)[:, None], (rows, GRID_W)).reshape(seq)
    pos_c = jnp.broadcast_to(jnp.arange(GRID_W, dtype=F32)[None, :], (rows, GRID_W)).reshape(seq)
    n = DA_DIM // 4
    inv = ROPE_BASE ** (-jnp.arange(n, dtype=F32) / n)
    ang = jnp.concatenate([pos_r[:, None] * inv, pos_c[:, None] * inv], axis=-1)
    cos, sin = jnp.cos(ang), jnp.sin(ang)
    one, zero = jnp.ones_like(cos), jnp.zeros_like(cos)
    cos_t = jnp.concatenate([{"a": cos, "b": cos, "i": one}[c] for c in pattern], axis=1)
    sin_t = jnp.concatenate([{"a": -sin, "b": sin, "i": zero}[c] for c in pattern], axis=1)
    pad_c = jnp.ones((ROW_TILE, cos_t.shape[1]), F32)
    pad_s = jnp.zeros((ROW_TILE, cos_t.shape[1]), F32)
    return jnp.concatenate([cos_t, pad_c], axis=0) * scale, jnp.concatenate([sin_t, pad_s], axis=0) * scale


def kernel(x, c, ctx, c_ctx, mod_w, mod_b, norm_mix, norm_ffn, ffn_w1, ffn_w2, attn_w_in, mla_g_q, mla_w_uq, mla_g_kv, mla_w_ukv, da_lam_q1, da_lam_k1, da_lam_q2, da_lam_k2, da_g_sub, attn_w_out, ssd_w_in, ssd_conv_w, ssd_conv_b, ssd_dt_bias, ssd_a_log, ssd_d, ssd_g_norm, ssd_w_out, final_g):
    bsz, seq, d = x.shape
    ctx_len = ctx.shape[1]
    depth = mod_w.shape[0]
    assert seq % ROW_TILE == 0 and bsz + 1 <= MOD_ROWS
    assert seq % ATT_TK == 0 and ctx_len % ATT_TQ == 0 and ctx_len % ATT_KC == 0 and seq % GRID_W == 0
    assert seq % CONV_ROWS == 0 and ctx_len % CONV_ROWS == 0 and seq % SSD_CHUNK == 0 and ctx_len % SSD_CHUNK == 0
    lay = dict(batch=bsz, seq=seq, ctx=ctx_len, tiles_per_seq=seq // ROW_TILE, lat_tiles=bsz * seq // ROW_TILE)
    n_lat = bsz * seq
    lat_tiles = n_lat // ROW_TILE

    n_pad = -(n_lat + bsz * ctx_len) % ROW_TILE
    xs = jnp.concatenate([x.reshape(n_lat, d), ctx.reshape(bsz * ctx_len, d), jnp.zeros((n_pad, d), F32)], axis=0)
    cvec = jnp.zeros((MOD_ROWS, d), F32).at[:bsz].set(c).at[bsz].set(c_ctx)
    mods = _modulation(cvec, mod_w, mod_b)

    cq, sq = _rope_tables(seq, "abab", DA_DIM ** -0.5 * LOG2E)
    ck, sk = _rope_tables(seq, "abab")
    cv, sv = _rope_tables(seq, "iiii")
    cos_da, sin_da = jnp.concatenate([cq, ck, cv], axis=0), jnp.concatenate([sq, sk, sv], axis=0)
    cos_kr, sin_kr = _rope_tables(seq, "abii")

    inner = ssd_w_out.shape[1]
    n_heads = inner // SSD_HEADDIM

    for l in range(depth):
        last = l == depth - 1
        i = l // 2
        mix_tiles = None
        out_tiles = lat_tiles if last else None
        pre_mix = dict(pre_gain=norm_mix[l][None, :], pre_mod=(mods, l * 6 + 0, l * 6 + 1))
        if l % 2 == 0:
            lambda_init = 0.8 - 0.6 * math.exp(-0.3 * l)
            c0, n_c = 3 * DA_COLS, MLA_Q_RANK + MLA_KV_RANK
            w_qkv = _cast_bf16(attn_w_in, i, 0, c0)
            w_ckr = jnp.pad(attn_w_in[i][:, c0:], ((0, 0), (0, LANE - MLA_ROPE))).astype(BF16)
            mla_scale = (MLA_NOPE + MLA_ROPE) ** -0.5 * LOG2E
            w_uq = jnp.pad((mla_w_uq[i] * mla_scale).reshape(MLA_Q_RANK, MLA_HEADS, MLA_NOPE + MLA_ROPE),
                           ((0, 0), (0, 0), (0, MLA_QK_PAD - MLA_NOPE - MLA_ROPE)))
            w_uq = w_uq.reshape(MLA_Q_RANK, MLA_HEADS * MLA_QK_PAD).astype(BF16)
            w_ukv = mla_w_ukv[i].reshape(MLA_KV_RANK, MLA_HEADS, MLA_NOPE + MLA_V)
            w_ukv = jnp.concatenate([w_ukv[:, :, :MLA_NOPE].reshape(MLA_KV_RANK, -1),
                                     w_ukv[:, :, MLA_NOPE:].reshape(MLA_KV_RANK, -1)], axis=1).astype(BF16)

            qkv_a = _linear(xs, w_qkv, tn=1024, out_dtype=BF16, lay=lay, name="attn_in_qkv", m_tiles=mix_tiles,
                            epi="rope", epi_args=(cos_da, sin_da, True, None), **pre_mix)
            cq_ckv = _linear(xs, w_ckr, tn=n_c + LANE, out_dtype=F32, lay=lay, name="attn_in_c", m_tiles=mix_tiles,
                             epi="rope", epi_args=(cos_kr, sin_kr, False, (n_c, n_c + LANE, LANE)), **pre_mix)
            q_m = _linear(cq_ckv, w_uq, tn=MLA_HEADS * MLA_QK_PAD, out_dtype=BF16, lay=lay, name="mla_uq",
                          k=MLA_Q_RANK, x_col_block=0, pre_gain=mla_g_q[i][None, :],
                          epi="rope", epi_args=(cos_kr, sin_kr, False, (MLA_NOPE, MLA_HEADS * MLA_QK_PAD, MLA_QK_PAD)))
            k_m, v_m = _linear(cq_ckv, w_ukv, tn=MLA_HEADS * (MLA_NOPE + MLA_V), out_dtype=BF16, lay=lay, name="mla_ukv",
                               k=MLA_KV_RANK, x_col_block=MLA_Q_RANK // MLA_KV_RANK, pre_gain=mla_g_kv[i][None, :],
                               epi="mla_kv", epi_args=(cq_ckv, n_c // LANE))
            lam_vecs = jnp.stack([da_lam_q1[i], da_lam_k1[i], da_lam_q2[i], da_lam_k2[i]]).astype(F32)
            o_att = _attention(qkv_a, q_m, k_m, v_m, lam_vecs, da_g_sub[i][None, :].astype(F32),
                               lay=lay, lambda_init=lambda_init)
            xs = _linear(o_att, _cast_bf16(attn_w_out, i), tn=1024, out_dtype=F32, lay=lay, name="attn_out",
                         m_tiles=out_tiles, epi="gate_res", epi_args=(xs, mods, l * 6 + 2))
        else:
            n_zx = inner + inner + 2 * SSD_GROUPS * SSD_STATE
            zx = _linear(xs, _cast_bf16(ssd_w_in, i, 0, n_zx), tn=1024, out_dtype=BF16, lay=lay, name="ssd_in_zx",
                         m_tiles=mix_tiles, **pre_mix)
            dt = _linear(xs, _cast_bf16(ssd_w_in, i, n_zx, 2 * n_heads), tn=2 * n_heads, out_dtype=F32, lay=lay, name="ssd_in_dt",
                         m_tiles=mix_tiles, epi="softplus", epi_args=(ssd_dt_bias[i].reshape(1, -1).astype(F32),),
                         **pre_mix)
            xbc = _ssd_conv(zx, ssd_conv_w[i].astype(F32), ssd_conv_b[i][None, :].astype(F32), lay=lay, col0=inner)
            a_log = ssd_a_log[i].reshape(1, -1).astype(F32)
            y_f = _ssd_scan(xbc, dt, a_log, lay=lay, direction=0)
            dsk = jnp.repeat(ssd_d[i].astype(F32), SSD_HEADDIM)[None, :]
            y = _ssd_scan(xbc, dt, a_log, lay=lay, direction=1,
                          finish_args=(y_f, zx, dsk, ssd_g_norm[i][None, :].astype(F32)))
            xs = _linear(y, _cast_bf16(ssd_w_out, i), tn=1024, out_dtype=F32, lay=lay, name="ssd_out",
                         m_tiles=out_tiles, epi="gate_res", epi_args=(xs, mods, l * 6 + 2))
        hid = _linear(xs, _cast_bf16(ffn_w1, l), tn=1024, out_dtype=BF16, lay=lay, name="ffn_up", m_tiles=out_tiles,
                      pre_gain=norm_ffn[l][None, :], pre_mod=(mods, l * 6 + 3, l * 6 + 4), epi="relu2")
        xs = _linear(hid, _cast_bf16(ffn_w2, l), tn=256, out_dtype=F32, lay=lay, name="ffn_down", m_tiles=out_tiles,
                     epi="gate_res", epi_args=(xs, mods, l * 6 + 5))

    return _final_norm(xs, final_g[None, :].astype(F32), n_lat).reshape(bsz, seq, d)
```

```python
import functools
import math

import jax
import jax.numpy as jnp
from jax import lax
from jax.experimental import pallas as pl
from jax.experimental.pallas import tpu as pltpu

F32 = jnp.float32
BF16 = jnp.bfloat16

EPS = 1e-6
ROPE_BASE = 10000.0
GRID_W = 64
LOG2E = 1.4426950408889634

DA_HEADS = 8
DA_DIM = 64
DA_COLS = DA_HEADS * 2 * DA_DIM
MLA_HEADS = 8
MLA_NOPE = 128
MLA_ROPE = 64
MLA_V = 128
MLA_Q_RANK = 512
MLA_KV_RANK = 256
MLA_QK_PAD = 256
SSD_HEADDIM = 64
SSD_GROUPS = 8
SSD_STATE = 128
SSD_CONV = 5
SSD_CHUNK = 128

LANE = 128
ROW_TILE = 1024
ATT_TQ = 256
ATT_TK = 1024
ATT_KC = 128
ATT_UNROLL = 8
CONV_ROWS = 256
CONV_BLOCK = 128
CONV_LANES = 256
HALO = 16
MOD_ROWS = 8
VMEM_LIMIT = 56 * 1024 * 1024


def _silu(x):
    return x * (1.0 / (1.0 + jnp.exp(-x)))


def _params(sem):
    return pltpu.CompilerParams(dimension_semantics=sem, vmem_limit_bytes=VMEM_LIMIT)


def _mod_kernel(c_ref, w_ref, b_ref, o_ref):
    s = _silu(c_ref[...]).astype(BF16)
    o_ref[0] = jnp.dot(s, w_ref[0].astype(BF16), preferred_element_type=F32) + b_ref[0]


def _modulation(cvec, mod_w, mod_b):
    depth, d, _ = mod_w.shape
    tn = min(1024, d)
    nj = d // tn
    return pl.pallas_call(
        _mod_kernel,
        grid=(depth, 6, nj),
        in_specs=[
            pl.BlockSpec((MOD_ROWS, d), lambda l, r, j: (0, 0)),
            pl.BlockSpec((1, d, tn), lambda l, r, j: (l, 0, r * nj + j)),
            pl.BlockSpec((1, 1, tn), lambda l, r, j: (l * 6 + r, 0, j)),
        ],
        out_specs=pl.BlockSpec((1, MOD_ROWS, tn), lambda l, r, j: (l * 6 + r, 0, j)),
        out_shape=jax.ShapeDtypeStruct((depth * 6, MOD_ROWS, d), F32),
        compiler_params=_params(("arbitrary", "arbitrary", "arbitrary")),
        name="modulation",
    )(cvec, mod_w, mod_b.reshape(depth * 6, 1, d))


def _cast_kernel(w_ref, o_ref):
    o_ref[...] = w_ref[0].astype(o_ref.dtype)


def _cast_bf16(w, layer, col0=0, ncols=None):
    _, k, n = w.shape
    ncols = n - col0 if ncols is None else ncols
    tn = ncols if ncols <= 2048 else 1024
    tk = min(k, 1024)
    assert ncols % tn == 0 and col0 % tn == 0 and k % tk == 0
    cb0 = col0 // tn
    return pl.pallas_call(
        _cast_kernel,
        grid=(k // tk, ncols // tn),
        in_specs=[pl.BlockSpec((1, tk, tn), lambda i, j: (layer, i, cb0 + j))],
        out_specs=pl.BlockSpec((tk, tn), lambda i, j: (i, j)),
        out_shape=jax.ShapeDtypeStruct((k, ncols), BF16),
        compiler_params=_params(("arbitrary", "arbitrary")),
        name="cast_bf16",
    )(w)


def _linear_kernel(*refs, pre, pre_mod, epi, tiles_per_seq, n_batch, tn, period, rope_cols, w_stacked):
    it = iter(refs)
    x_ref, w_ref = next(it), next(it)
    g_ref = next(it) if pre else None
    sh_ref = next(it) if pre_mod else None
    sc_ref = next(it) if pre_mod else None
    if epi == "softplus":
        bias_ref = next(it)
    elif epi == "rope":
        cos_ref, sin_ref = next(it), next(it)
    elif epi == "gate_res":
        res_ref, gate_ref = next(it), next(it)
    elif epi == "mla_kv":
        kr_ref = next(it)
    o_ref = next(it)
    o2_ref = next(it) if epi == "mla_kv" else None
    h_ref = next(it) if pre else None

    grp = jnp.minimum(pl.program_id(0) // tiles_per_seq, n_batch)

    if pre:
        @pl.when(pl.program_id(1) == 0)
        def _():
            x32 = x_ref[...].astype(F32)
            ms = jnp.mean(x32 * x32, axis=-1, keepdims=True)
            y = x32 * lax.rsqrt(ms + EPS) * g_ref[...]
            if pre_mod:
                y = y * (1.0 + sc_ref[0, pl.ds(grp, 1), :]) + sh_ref[0, pl.ds(grp, 1), :]
            h_ref[...] = y.astype(BF16)
        lhs = h_ref[...]
    else:
        lhs = x_ref[...]

    w = w_ref[0].astype(BF16) if w_stacked else w_ref[...]
    acc = jnp.dot(lhs, w, preferred_element_type=F32)

    if epi == "none":
        o_ref[...] = acc.astype(o_ref.dtype)
    elif epi == "relu2":
        r = jnp.maximum(acc, 0.0)
        o_ref[...] = (r * r).astype(o_ref.dtype)
    elif epi == "softplus":
        v = acc + bias_ref[...]
        o_ref[...] = (jnp.maximum(v, 0.0) + jnp.log(1.0 + jnp.exp(-jnp.abs(v)))).astype(o_ref.dtype)
    elif epi == "rope":
        cos, sin = cos_ref[...], sin_ref[...]
        lane = lax.broadcasted_iota(jnp.int32, cos.shape, 1)
        first = (lane % (2 * 32)) < 32
        for s in range(tn // period):
            a = acc[:, s * period:(s + 1) * period]
            if rope_cols is None or (rope_cols[0] <= s * period < rope_cols[1]
                                     and (s * period - rope_cols[0]) % rope_cols[2] == 0):
                partner = jnp.where(first, pltpu.roll(a, period - 32, 1), pltpu.roll(a, 32, 1))
                a = a * cos + partner * sin
            o_ref[:, s * period:(s + 1) * period] = a.astype(o_ref.dtype)
    elif epi == "gate_res":
        o_ref[...] = res_ref[...] + gate_ref[0, pl.ds(grp, 1), :] * acc
    elif epi == "mla_kv":
        for h in range(MLA_HEADS):
            o_ref[:, h * MLA_QK_PAD:h * MLA_QK_PAD + MLA_NOPE] = acc[:, h * MLA_NOPE:(h + 1) * MLA_NOPE].astype(o_ref.dtype)
            o_ref[:, h * MLA_QK_PAD + MLA_NOPE:(h + 1) * MLA_QK_PAD] = kr_ref[...].astype(o_ref.dtype)
        o2_ref[...] = acc[:, MLA_HEADS * MLA_NOPE:].astype(o2_ref.dtype)


def _linear(x, w, *, tn, out_dtype, lay, name, k=None, x_col_block=0, m_tiles=None, w_layer=None, n=None,
            pre_gain=None, pre_mod=None, epi="none", epi_args=()):
    m = x.shape[0]
    k = x.shape[1] if k is None else k
    n = w.shape[-1] if n is None else n
    tm = ROW_TILE
    mt = m // tm if m_tiles is None else m_tiles
    nt = n // tn
    assert n % tn == 0 and w.shape[-2] == k
    pre = pre_gain is not None
    period = rope_cols = None

    args = [x, w]
    specs = [pl.BlockSpec((tm, k), lambda i, j: (i, x_col_block)),
             pl.BlockSpec((k, tn), lambda i, j: (0, j)) if w_layer is None else
             pl.BlockSpec((1, k, tn), lambda i, j: (w_layer, 0, j))]
    if pre:
        args.append(pre_gain)
        specs.append(pl.BlockSpec((1, k), lambda i, j: (0, 0)))
    if pre_mod is not None:
        mods, shift_idx, scale_idx = pre_mod
        args += [mods, mods]
        specs += [pl.BlockSpec((1, MOD_ROWS, k), lambda i, j: (shift_idx, 0, 0)),
                  pl.BlockSpec((1, MOD_ROWS, k), lambda i, j: (scale_idx, 0, 0))]
    if epi == "softplus":
        (bias,) = epi_args
        args.append(bias)
        specs.append(pl.BlockSpec((1, tn), lambda i, j: (0, j)))
    elif epi == "rope":
        cos, sin, per_col, rope_cols = epi_args
        period = cos.shape[1]
        n_lat, tps = lay["lat_tiles"], lay["tiles_per_seq"]
        tab_map = lambda i, j: (jnp.where(i < n_lat, i % tps, tps) + (j * (tps + 1) if per_col else 0), 0)
        args += [cos, sin]
        specs += [pl.BlockSpec((tm, period), tab_map), pl.BlockSpec((tm, period), tab_map)]
    elif epi == "gate_res":
        res, mods, gate_idx = epi_args
        args += [res, mods]
        specs += [pl.BlockSpec((tm, tn), lambda i, j: (i, j)),
                  pl.BlockSpec((1, MOD_ROWS, tn), lambda i, j: (gate_idx, 0, j))]
    out_specs = pl.BlockSpec((tm, tn), lambda i, j: (i, j))
    out_shape = jax.ShapeDtypeStruct((mt * tm, n), out_dtype)
    if epi == "mla_kv":
        k_rope, kr_col = epi_args
        assert nt == 1 and n == MLA_HEADS * (MLA_NOPE + MLA_V)
        args.append(k_rope)
        specs.append(pl.BlockSpec((tm, LANE), lambda i, j: (i, kr_col)))
        out_specs = [pl.BlockSpec((tm, MLA_HEADS * MLA_QK_PAD), lambda i, j: (i, 0)),
                     pl.BlockSpec((tm, MLA_HEADS * MLA_V), lambda i, j: (i, 0))]
        out_shape = [jax.ShapeDtypeStruct((mt * tm, MLA_HEADS * MLA_QK_PAD), out_dtype),
                     jax.ShapeDtypeStruct((mt * tm, MLA_HEADS * MLA_V), out_dtype)]

    kern = functools.partial(_linear_kernel, pre=pre, pre_mod=pre_mod is not None, epi=epi,
                             tiles_per_seq=lay["tiles_per_seq"], n_batch=lay["batch"], tn=tn, period=period,
                             rope_cols=rope_cols, w_stacked=w_layer is not None)
    return pl.pallas_call(
        kern,
        grid=(mt, nt),
        in_specs=specs,
        out_specs=out_specs,
        out_shape=out_shape,
        scratch_shapes=[pltpu.VMEM((tm, k), BF16)] if pre else [],
        compiler_params=_params(("arbitrary", "arbitrary")),
        name=name,
    )(*args)


def _attn_kernel(lam_ref, gsub_ref, qa_ref, qm_ref, kac_ref, vac_ref, kmc_ref, vmc_ref,
                 ka_ref, va_ref, km_ref, vm_ref, o_ref, qta_ref, qtm_ref, m_ref, l_ref, acc_ref,
                 *, lambda_init, n_steps, n_lat_q):
    qi, t = pl.program_id(0), pl.program_id(1)
    tq = qa_ref.shape[0]
    ctx_len, tk, kc = kac_ref.shape[0], ka_ref.shape[0], ATT_KC

    @pl.when(t == 0)
    def _init():
        m_ref[...] = jnp.full(m_ref.shape, -jnp.inf, F32)
        l_ref[...] = jnp.zeros(l_ref.shape, F32)
        acc_ref[...] = jnp.zeros(acc_ref.shape, F32)
        first = lax.broadcasted_iota(jnp.int32, (LANE, tq), 0) < DA_DIM
        for h in range(DA_HEADS):
            qt = qa_ref[:, h * LANE:(h + 1) * LANE].astype(F32).T
            qta_ref[2 * h] = jnp.where(first, qt, 0.0).astype(BF16)
            qta_ref[2 * h + 1] = jnp.where(first, 0.0, qt).astype(BF16)
        for h in range(MLA_HEADS):
            qtm_ref[h] = qm_ref[:, h * MLA_QK_PAD:(h + 1) * MLA_QK_PAD].astype(F32).T.astype(BF16)

    def update(idx, qt, k, v):
        s = jnp.dot(k, qt, preferred_element_type=F32)
        m_prev = m_ref[idx]
        m_next = jnp.maximum(m_prev, jnp.max(s, axis=0, keepdims=True))
        alpha = jnp.exp2(m_prev - m_next)
        p = jnp.exp2(s - m_next)
        l_ref[idx] = alpha * l_ref[idx] + jnp.sum(p, axis=0, keepdims=True)
        pv = lax.dot_general(v, p.astype(BF16), (((0,), (0,)), ((), ())), preferred_element_type=F32)
        acc_ref[idx] = alpha * acc_ref[idx] + pv
        m_ref[idx] = m_next

    def chunk(ka, va, km, vm, r0):
        rows = pl.ds(r0, kc)
        for h in range(DA_HEADS):
            k = ka[rows, h * LANE:(h + 1) * LANE]
            v = va[rows, h * LANE:(h + 1) * LANE]
            for c in range(2):
                update(2 * h + c, qta_ref[2 * h + c], k, v)
        for h in range(MLA_HEADS):
            update(2 * DA_HEADS + h, qtm_ref[h], km[rows, h * MLA_QK_PAD:(h + 1) * MLA_QK_PAD],
                   vm[rows, h * MLA_V:(h + 1) * MLA_V])

    @pl.when(t == 0)
    def _ctx():
        def body(r, carry):
            chunk(kac_ref, vac_ref, kmc_ref, vmc_ref, pl.multiple_of(r * kc, kc))
            return carry
        lax.fori_loop(0, ctx_len // kc, body, 0, unroll=True)

    @pl.when(qi < n_lat_q)
    def _lat():
        def body(r, carry):
            chunk(ka_ref, va_ref, km_ref, vm_ref, pl.multiple_of(r * kc, kc))
            return carry
        lax.fori_loop(0, tk // kc, body, 0, unroll=ATT_UNROLL)

    @pl.when(t == n_steps - 1)
    def _finish():
        lv = lam_ref[...]
        lam = (jnp.exp(jnp.sum(lv[0:1] * lv[1:2], axis=1, keepdims=True))
               - jnp.exp(jnp.sum(lv[2:3] * lv[3:4], axis=1, keepdims=True)) + lambda_init)
        for h in range(DA_HEADS):
            o = acc_ref[2 * h] / l_ref[2 * h] - lam * (acc_ref[2 * h + 1] / l_ref[2 * h + 1])
            ms = jnp.mean(o * o, axis=0, keepdims=True)
            o = (o * lax.rsqrt(ms + EPS)).T * gsub_ref[...] * (1.0 - lambda_init)
            o_ref[:, h * LANE:(h + 1) * LANE] = o.astype(o_ref.dtype)
        for h in range(MLA_HEADS):
            idx = 2 * DA_HEADS + h
            o_ref[:, DA_COLS + h * MLA_V:DA_COLS + (h + 1) * MLA_V] = (acc_ref[idx] / l_ref[idx]).T.astype(o_ref.dtype)


def _attention(qkv_a, q_m, k_m, v_m, lam_vecs, g_sub, *, lay, lambda_init):
    m = qkv_a.shape[0]
    tq, tk = ATT_TQ, ATT_TK
    s_len, ctx_len, bsz = lay["seq"], lay["ctx"], lay["batch"]
    n_lat_q = bsz * s_len // tq
    n_ctx_q = (m - bsz * s_len) // tq
    q_per_seq, q_per_ctx = s_len // tq, ctx_len // tq
    n_steps = s_len // tk
    ctx_base = bsz * s_len // ctx_len

    def batch_of(qi):
        return jnp.where(qi < n_lat_q, qi // q_per_seq, jnp.minimum((qi - n_lat_q) // q_per_ctx, bsz - 1))

    ctx_row = lambda col: (lambda qi, t: (ctx_base + batch_of(qi), col))
    lat_row = lambda col: (lambda qi, t: (batch_of(qi) * n_steps + jnp.where(qi < n_lat_q, t, 0), col))

    n_state = 2 * DA_HEADS + MLA_HEADS
    kern = functools.partial(_attn_kernel, lambda_init=lambda_init, n_steps=n_steps, n_lat_q=n_lat_q)
    return pl.pallas_call(
        kern,
        grid=(n_lat_q + n_ctx_q, n_steps),
        in_specs=[
            pl.BlockSpec((4, DA_DIM), lambda qi, t: (0, 0)),
            pl.BlockSpec((1, 2 * DA_DIM), lambda qi, t: (0, 0)),
            pl.BlockSpec((tq, DA_COLS), lambda qi, t: (qi, 0)),
            pl.BlockSpec((tq, MLA_HEADS * MLA_QK_PAD), lambda qi, t: (qi, 0)),
            pl.BlockSpec((ctx_len, DA_COLS), ctx_row(1)),
            pl.BlockSpec((ctx_len, DA_COLS), ctx_row(2)),
            pl.BlockSpec((ctx_len, MLA_HEADS * MLA_QK_PAD), ctx_row(0)),
            pl.BlockSpec((ctx_len, MLA_HEADS * MLA_V), ctx_row(0)),
            pl.BlockSpec((tk, DA_COLS), lat_row(1)),
            pl.BlockSpec((tk, DA_COLS), lat_row(2)),
            pl.BlockSpec((tk, MLA_HEADS * MLA_QK_PAD), lat_row(0)),
            pl.BlockSpec((tk, MLA_HEADS * MLA_V), lat_row(0)),
        ],
        out_specs=pl.BlockSpec((tq, DA_COLS + MLA_HEADS * MLA_V), lambda qi, t: (qi, 0)),
        out_shape=jax.ShapeDtypeStruct((m, DA_COLS + MLA_HEADS * MLA_V), BF16),
        scratch_shapes=[
            pltpu.VMEM((2 * DA_HEADS, LANE, tq), BF16),
            pltpu.VMEM((MLA_HEADS, MLA_QK_PAD, tq), BF16),
            pltpu.VMEM((n_state, 1, tq), F32),
            pltpu.VMEM((n_state, 1, tq), F32),
            pltpu.VMEM((n_state, LANE, tq), F32),
        ],
        compiler_params=_params(("arbitrary", "arbitrary")),
        name="attention",
    )(lam_vecs, g_sub, qkv_a, q_m, qkv_a, qkv_a, k_m, v_m, qkv_a, qkv_a, k_m, v_m)


def _conv_kernel(xm_ref, xp_ref, xn_ref, w_ref, b_ref, s_ref, o_ref, *, tm, seq, ctx, n_lat_rows):
    row0 = pl.program_id(0) * tm
    in_lat = row0 < n_lat_rows
    local = jnp.where(in_lat, row0 % seq, (row0 - n_lat_rows) % ctx)
    seq_len = jnp.where(in_lat, seq, ctx)
    keep_prev = (local != 0).astype(BF16)
    keep_next = (local + tm != seq_len).astype(BF16)
    blk = CONV_BLOCK
    for r in range(tm // blk):
        prev = xp_ref[...] * keep_prev if r == 0 else xm_ref[r * blk - HALO:r * blk, :]
        nxt = xn_ref[...] * keep_next if r == tm // blk - 1 else xm_ref[(r + 1) * blk:(r + 1) * blk + HALO, :]
        x_ext = jnp.concatenate([prev, xm_ref[r * blk:(r + 1) * blk, :], nxt], axis=0)
        for cs in range(0, x_ext.shape[1], CONV_LANES):
            cols = slice(cs, cs + CONV_LANES)
            acc = jnp.broadcast_to(b_ref[:, cols], (blk, CONV_LANES))
            for kk in range(SSD_CONV):
                acc = acc + w_ref[kk:kk + 1, cols] * jnp.dot(s_ref[kk], x_ext[:, cols], preferred_element_type=F32)
            o_ref[r * blk:(r + 1) * blk, cols] = _silu(acc).astype(o_ref.dtype)


def _ssd_conv(zx, conv_w, conv_b, *, lay, col0):
    m = zx.shape[0]
    c = conv_w.shape[1]
    tm, tc = CONV_ROWS, 1024
    hb = tm // HALO
    last = m // HALO - 1
    cb0 = col0 // tc
    rows = lax.broadcasted_iota(jnp.int32, (SSD_CONV, CONV_BLOCK, CONV_BLOCK + 2 * HALO), 1)
    cols = lax.broadcasted_iota(jnp.int32, (SSD_CONV, CONV_BLOCK, CONV_BLOCK + 2 * HALO), 2)
    taps = lax.broadcasted_iota(jnp.int32, (SSD_CONV, CONV_BLOCK, CONV_BLOCK + 2 * HALO), 0)
    shifts = (cols == rows + HALO - SSD_CONV // 2 + taps).astype(BF16)
    kern = functools.partial(_conv_kernel, tm=tm, seq=lay["seq"], ctx=lay["ctx"], n_lat_rows=lay["batch"] * lay["seq"])
    return pl.pallas_call(
        kern,
        grid=(m // tm, c // tc),
        in_specs=[
            pl.BlockSpec((tm, tc), lambda i, j: (i, cb0 + j)),
            pl.BlockSpec((HALO, tc), lambda i, j: (jnp.maximum(i * hb - 1, 0), cb0 + j)),
            pl.BlockSpec((HALO, tc), lambda i, j: (jnp.minimum((i + 1) * hb, last), cb0 + j)),
            pl.BlockSpec((SSD_CONV, tc), lambda i, j: (0, j)),
            pl.BlockSpec((1, tc), lambda i, j: (0, j)),
            pl.BlockSpec(shifts.shape, lambda i, j: (0, 0, 0)),
        ],
        out_specs=pl.BlockSpec((tm, tc), lambda i, j: (i, j)),
        out_shape=jax.ShapeDtypeStruct((m, c), BF16),
        compiler_params=_params(("arbitrary", "arbitrary")),
        name="ssd_conv",
    )(zx, zx, zx, conv_w, conv_b, shifts)


def _ssd_scan_kernel(*refs, direction, finish, n_heads, n_batch, pad_rows):
    it = iter(refs)
    alog_ref, x_ref, b_ref, c_ref, dt_ref = (next(it) for _ in range(5))
    if finish:
        yf_ref, z_ref, dsk_ref, gn_ref = (next(it) for _ in range(4))
    o_ref = next(it)
    state_ref, lhs_ref, rhs_ref, bw_ref = (next(it) for _ in range(4))
    y_ref = next(it) if finish else o_ref

    L, N = SSD_CHUNK, SSD_STATE
    assert L == N == LANE
    heads_per_group = n_heads // SSD_GROUPS
    pairs_per_group = heads_per_group // 2

    def main():
        @pl.when(pl.program_id(1) == 0)
        def _():
            state_ref[...] = jnp.zeros(state_ref.shape, F32)

        a_rate = -jnp.exp(alog_ref[...])
        dt = dt_ref[...]
        ri = lax.broadcasted_iota(jnp.int32, (L, L), 0)
        ci = lax.broadcasted_iota(jnp.int32, (L, L), 1)
        tri = (ri >= ci) if direction == 0 else (ri <= ci)
        dta = dt * a_rate
        hi = dta.astype(BF16)
        r1 = dta - hi.astype(F32)
        mid = r1.astype(BF16)
        lo = (r1 - mid.astype(F32)).astype(BF16)
        tri_b = tri.astype(BF16)
        a = (jnp.dot(tri_b, hi, preferred_element_type=F32) + jnp.dot(tri_b, mid, preferred_element_type=F32)
             + jnp.dot(tri_b, lo, preferred_element_type=F32))
        a = a * LOG2E
        a_t = a.T
        dt_t = dt.T
        la_t = a_t - jnp.log2(dt_t)
        last = L - 1 if direction == 0 else 0
        first = lax.broadcasted_iota(jnp.int32, (1, LANE), 1) < SSD_HEADDIM

        for g in range(SSD_GROUPS):
            bg = b_ref[:, g * N:(g + 1) * N]
            cg = c_ref[:, g * N:(g + 1) * N]
            cb = lax.dot_general(cg, bg, (((1,), (1,)), ((), ())), preferred_element_type=F32)
            cg32 = cg.astype(F32)
            bg_t = bg.astype(F32).T
            for pp in range(pairs_per_group):
                p = g * pairs_per_group + pp
                xp = x_ref[:, p * LANE:(p + 1) * LANE]
                h_t = state_ref[p]
                dec = []
                for kk in range(2):
                    c = direction * n_heads + 2 * p + kk
                    tot = a_t[c:c + 1, last:last + 1]
                    a_b = jnp.broadcast_to(a[:, c:c + 1], (L, L))
                    seg_dt = jnp.exp2(jnp.where(tri, a_b - la_t[c:c + 1, :], -jnp.inf))
                    lhs_ref[p, :, kk * L:(kk + 1) * L] = (cb * seg_dt).astype(BF16)
                    lhs_ref[p, :, (2 + kk) * L:(3 + kk) * L] = (cg32 * jnp.exp2(a_b)).astype(BF16)
                    bw_ref[p, :, kk * L:(kk + 1) * L] = (bg_t * (jnp.exp2(tot - a_t[c:c + 1, :]) * dt_t[c:c + 1, :])).astype(BF16)
                    dec.append(jnp.exp2(tot))
                zero_x = jnp.zeros_like(xp)
                h_b = h_t.astype(BF16)
                zero_h = jnp.zeros_like(h_b)
                rhs_ref[p, 0:L] = jnp.where(first, xp, zero_x)
                rhs_ref[p, L:2 * L] = jnp.where(first, zero_x, xp)
                rhs_ref[p, 2 * L:3 * L] = jnp.where(first, h_b, zero_h)
                rhs_ref[p, 3 * L:4 * L] = jnp.where(first, zero_h, h_b)
                y = jnp.dot(lhs_ref[p], rhs_ref[p], preferred_element_type=F32)
                y_ref[:, p * LANE:(p + 1) * LANE] = y.astype(y_ref.dtype)
                upd = jnp.dot(bw_ref[p], rhs_ref[p, 0:2 * L], preferred_element_type=F32)
                state_ref[p] = h_t * jnp.where(first, dec[0], dec[1]) + upd

        if finish:
            gw = n_heads * SSD_HEADDIM // SSD_GROUPS
            for g in range(SSD_GROUPS):
                sl = slice(g * gw, (g + 1) * gw)
                v = yf_ref[:, sl].astype(F32) + y_ref[:, sl] + dsk_ref[:, sl] * x_ref[:, sl].astype(F32)
                v = v * _silu(z_ref[:, sl].astype(F32))
                ms = jnp.mean(v * v, axis=1, keepdims=True)
                o_ref[:, sl] = (v * lax.rsqrt(ms + EPS) * gn_ref[:, sl]).astype(o_ref.dtype)

    if pad_rows:
        pl.when(pl.program_id(0) < n_batch)(main)

        @pl.when(pl.program_id(0) >= n_batch)
        def _():
            o_ref[...] = jnp.zeros(o_ref.shape, o_ref.dtype)
    else:
        main()


def _ssd_scan(xbc, dt, a_log, *, lay, direction, finish_args=None):
    m = xbc.shape[0]
    L = SSD_CHUNK
    gn = SSD_GROUPS * SSD_STATE
    inner = xbc.shape[1] - 2 * gn
    n_heads = inner // SSD_HEADDIM
    bsz = lay["batch"]
    ncl, ncc = lay["seq"] // L, lay["ctx"] // L
    ctx_base = bsz * ncl
    finish = finish_args is not None

    n_real = bsz * (ncl + ncc)
    n_pad = (m // L - n_real) if finish else 0

    def chunk(b, t):
        if direction == 0:
            real = jnp.where(t < ncc, ctx_base + b * ncc + t, b * ncl + (t - ncc))
        else:
            real = jnp.where(t < ncc, ctx_base + b * ncc + (ncc - 1 - t), b * ncl + (ncl - 1 - (t - ncc)))
        if n_pad:
            return jnp.where(b < bsz, real, n_real + jnp.minimum(t, n_pad - 1))
        return real

    row = lambda b, t: (chunk(b, t), 0)
    args = [a_log, xbc, xbc, xbc, dt]
    specs = [
        pl.BlockSpec((1, 2 * n_heads), lambda b, t: (0, 0)),
        pl.BlockSpec((L, inner), row),
        pl.BlockSpec((L, gn), lambda b, t: (chunk(b, t), inner // gn)),
        pl.BlockSpec((L, gn), lambda b, t: (chunk(b, t), inner // gn + 1)),
        pl.BlockSpec((L, 2 * n_heads), row),
    ]
    n_pairs = n_heads // 2
    scratch = [pltpu.VMEM((n_pairs, SSD_STATE, LANE), F32),
               pltpu.VMEM((n_pairs, L, 4 * L), BF16),
               pltpu.VMEM((n_pairs, 4 * L, LANE), BF16),
               pltpu.VMEM((n_pairs, SSD_STATE, 2 * L), BF16)]
    if finish:
        y_f, zx, dsk, g_norm = finish_args
        args += [y_f, zx, dsk, g_norm]
        specs += [pl.BlockSpec((L, inner), row), pl.BlockSpec((L, inner), row),
                  pl.BlockSpec((1, inner), lambda b, t: (0, 0)), pl.BlockSpec((1, inner), lambda b, t: (0, 0))]
        scratch.append(pltpu.VMEM((L, inner), F32))
    kern = functools.partial(_ssd_scan_kernel, direction=direction, finish=finish, n_heads=n_heads,
                             n_batch=bsz, pad_rows=n_pad > 0)
    return pl.pallas_call(
        kern,
        grid=(bsz + (1 if n_pad else 0), ncc + ncl),
        in_specs=specs,
        out_specs=pl.BlockSpec((L, inner), row),
        out_shape=jax.ShapeDtypeStruct((m, inner), BF16),
        scratch_shapes=scratch,
        compiler_params=_params(("arbitrary", "arbitrary")),
        name="ssd_scan_bwd_finish" if finish else "ssd_scan_fwd",
    )(*args)


def _final_norm_kernel(x_ref, g_ref, o_ref):
    x = x_ref[...]
    ms = jnp.mean(x * x, axis=-1, keepdims=True)
    o_ref[...] = x * lax.rsqrt(ms + EPS) * g_ref[...]


def _final_norm(x, g, rows):
    d = x.shape[1]
    return pl.pallas_call(
        _final_norm_kernel,
        grid=(rows // ROW_TILE,),
        in_specs=[pl.BlockSpec((ROW_TILE, d), lambda i: (i, 0)), pl.BlockSpec((1, d), lambda i: (0, 0))],
        out_specs=pl.BlockSpec((ROW_TILE, d), lambda i: (i, 0)),
        out_shape=jax.ShapeDtypeStruct((rows, d), F32),
        compiler_params=_params(("arbitrary",)),
        name="final_norm",
    )(x, g)


def _rope_tables(seq, pattern, scale=1.0):
    rows = seq // GRID_W
    pos_r = jnp.broadcast_to(jnp.arange(rows, dtype=F32)[:, None], (rows, GRID_W)).reshape(seq)
    pos_c = jnp.broadcast_to(jnp.arange(GRID_W, dtype=F32)[None, :], (rows, GRID_W)).reshape(seq)
    n = DA_DIM // 4
    inv = ROPE_BASE ** (-jnp.arange(n, dtype=F32) / n)
    ang = jnp.concatenate([pos_r[:, None] * inv, pos_c[:, None] * inv], axis=-1)
    cos, sin = jnp.cos(ang), jnp.sin(ang)
    one, zero = jnp.ones_like(cos), jnp.zeros_like(cos)
    cos_t = jnp.concatenate([{"a": cos, "b": cos, "i": one}[c] for c in pattern], axis=1)
    sin_t = jnp.concatenate([{"a": -sin, "b": sin, "i": zero}[c] for c in pattern], axis=1)
    pad_c = jnp.ones((ROW_TILE, cos_t.shape[1]), F32)
    pad_s = jnp.zeros((ROW_TILE, cos_t.shape[1]), F32)
    return jnp.concatenate([cos_t, pad_c], axis=0) * scale, jnp.concatenate([sin_t, pad_s], axis=0) * scale


def kernel(x, c, ctx, c_ctx, mod_w, mod_b, norm_mix, norm_ffn, ffn_w1, ffn_w2, attn_w_in, mla_g_q, mla_w_uq, mla_g_kv, mla_w_ukv, da_lam_q1, da_lam_k1, da_lam_q2, da_lam_k2, da_g_sub, attn_w_out, ssd_w_in, ssd_conv_w, ssd_conv_b, ssd_dt_bias, ssd_a_log, ssd_d, ssd_g_norm, ssd_w_out, final_g):
    bsz, seq, d = x.shape
    ctx_len = ctx.shape[1]
    depth = mod_w.shape[0]
    assert seq % ROW_TILE == 0 and bsz + 1 <= MOD_ROWS
    assert seq % ATT_TK == 0 and ctx_len % ATT_TQ == 0 and ctx_len % ATT_KC == 0 and seq % GRID_W == 0
    assert seq % CONV_ROWS == 0 and ctx_len % CONV_ROWS == 0 and seq % SSD_CHUNK == 0 and ctx_len % SSD_CHUNK == 0
    lay = dict(batch=bsz, seq=seq, ctx=ctx_len, tiles_per_seq=seq // ROW_TILE, lat_tiles=bsz * seq // ROW_TILE)
    n_lat = bsz * seq
    lat_tiles = n_lat // ROW_TILE

    n_pad = -(n_lat + bsz * ctx_len) % ROW_TILE
    xs = jnp.concatenate([x.reshape(n_lat, d), ctx.reshape(bsz * ctx_len, d), jnp.zeros((n_pad, d), F32)], axis=0)
    cvec = jnp.zeros((MOD_ROWS, d), F32).at[:bsz].set(c).at[bsz].set(c_ctx)
    mods = _modulation(cvec, mod_w, mod_b)

    cq, sq = _rope_tables(seq, "abab", DA_DIM ** -0.5 * LOG2E)
    ck, sk = _rope_tables(seq, "abab")
    cv, sv = _rope_tables(seq, "iiii")
    cos_da, sin_da = jnp.concatenate([cq, ck, cv], axis=0), jnp.concatenate([sq, sk, sv], axis=0)
    cos_kr, sin_kr = _rope_tables(seq, "abii")

    inner = ssd_w_out.shape[1]
    n_heads = inner // SSD_HEADDIM

    for l in range(depth):
        last = l == depth - 1
        i = l // 2
        mix_tiles = None
        out_tiles = lat_tiles if last else None
        pre_mix = dict(pre_gain=norm_mix[l][None, :], pre_mod=(mods, l * 6 + 0, l * 6 + 1))
        if l % 2 == 0:
            lambda_init = 0.8 - 0.6 * math.exp(-0.3 * l)
            c0, n_c = 3 * DA_COLS, MLA_Q_RANK + MLA_KV_RANK
            w_ckr = jnp.pad(attn_w_in[i][:, c0:], ((0, 0), (0, LANE - MLA_ROPE))).astype(BF16)
            mla_scale = (MLA_NOPE + MLA_ROPE) ** -0.5 * LOG2E
            w_uq = jnp.pad((mla_w_uq[i] * mla_scale).reshape(MLA_Q_RANK, MLA_HEADS, MLA_NOPE + MLA_ROPE),
                           ((0, 0), (0, 0), (0, MLA_QK_PAD - MLA_NOPE - MLA_ROPE)))
            w_uq = w_uq.reshape(MLA_Q_RANK, MLA_HEADS * MLA_QK_PAD).astype(BF16)
            w_ukv = mla_w_ukv[i].reshape(MLA_KV_RANK, MLA_HEADS, MLA_NOPE + MLA_V)
            w_ukv = jnp.concatenate([w_ukv[:, :, :MLA_NOPE].reshape(MLA_KV_RANK, -1),
                                     w_ukv[:, :, MLA_NOPE:].reshape(MLA_KV_RANK, -1)], axis=1).astype(BF16)

            qkv_a = _linear(xs, attn_w_in, w_layer=i, n=c0, tn=1024, out_dtype=BF16, lay=lay, name="attn_in_qkv", m_tiles=mix_tiles,
                            epi="rope", epi_args=(cos_da, sin_da, True, None), **pre_mix)
            cq_ckv = _linear(xs, w_ckr, tn=n_c + LANE, out_dtype=F32, lay=lay, name="attn_in_c", m_tiles=mix_tiles,
                             epi="rope", epi_args=(cos_kr, sin_kr, False, (n_c, n_c + LANE, LANE)), **pre_mix)
            q_m = _linear(cq_ckv, w_uq, tn=MLA_HEADS * MLA_QK_PAD, out_dtype=BF16, lay=lay, name="mla_uq",
                          k=MLA_Q_RANK, x_col_block=0, pre_gain=mla_g_q[i][None, :],
                          epi="rope", epi_args=(cos_kr, sin_kr, False, (MLA_NOPE, MLA_HEADS * MLA_QK_PAD, MLA_QK_PAD)))
            k_m, v_m = _linear(cq_ckv, w_ukv, tn=MLA_HEADS * (MLA_NOPE + MLA_V), out_dtype=BF16, lay=lay, name="mla_ukv",
                               k=MLA_KV_RANK, x_col_block=MLA_Q_RANK // MLA_KV_RANK, pre_gain=mla_g_kv[i][None, :],
                               epi="mla_kv", epi_args=(cq_ckv, n_c // LANE))
            lam_vecs = jnp.stack([da_lam_q1[i], da_lam_k1[i], da_lam_q2[i], da_lam_k2[i]]).astype(F32)
            o_att = _attention(qkv_a, q_m, k_m, v_m, lam_vecs, da_g_sub[i][None, :].astype(F32),
                               lay=lay, lambda_init=lambda_init)
            xs = _linear(o_att, _cast_bf16(attn_w_out, i), tn=1024, out_dtype=F32, lay=lay, name="attn_out",
                         m_tiles=out_tiles, epi="gate_res", epi_args=(xs, mods, l * 6 + 2))
        else:
            n_zx = inner + inner + 2 * SSD_GROUPS * SSD_STATE
            zx = _linear(xs, ssd_w_in, w_layer=i, n=n_zx, tn=1024, out_dtype=BF16, lay=lay, name="ssd_in_zx",
                         m_tiles=mix_tiles, **pre_mix)
            dt = _linear(xs, _cast_bf16(ssd_w_in, i, n_zx, 2 * n_heads), tn=2 * n_heads, out_dtype=F32, lay=lay, name="ssd_in_dt",
                         m_tiles=mix_tiles, epi="softplus", epi_args=(ssd_dt_bias[i].reshape(1, -1).astype(F32),),
                         **pre_mix)
            xbc = _ssd_conv(zx, ssd_conv_w[i].astype(F32), ssd_conv_b[i][None, :].astype(F32), lay=lay, col0=inner)
            a_log = ssd_a_log[i].reshape(1, -1).astype(F32)
            y_f = _ssd_scan(xbc, dt, a_log, lay=lay, direction=0)
            dsk = jnp.repeat(ssd_d[i].astype(F32), SSD_HEADDIM)[None, :]
            y = _ssd_scan(xbc, dt, a_log, lay=lay, direction=1,
                          finish_args=(y_f, zx, dsk, ssd_g_norm[i][None, :].astype(F32)))
            xs = _linear(y, _cast_bf16(ssd_w_out, i), tn=1024, out_dtype=F32, lay=lay, name="ssd_out",
                         m_tiles=out_tiles, epi="gate_res", epi_args=(xs, mods, l * 6 + 2))
        hid = _linear(xs, ffn_w1, w_layer=l, tn=1024, out_dtype=BF16, lay=lay, name="ffn_up", m_tiles=out_tiles,
                      pre_gain=norm_ffn[l][None, :], pre_mod=(mods, l * 6 + 3, l * 6 + 4), epi="relu2")
        xs = _linear(hid, _cast_bf16(ffn_w2, l), tn=256, out_dtype=F32, lay=lay, name="ffn_down", m_tiles=out_tiles,
                     epi="gate_res", epi_args=(xs, mods, l * 6 + 5))

    return _final_norm(xs, final_g[None, :].astype(F32), n_lat).reshape(bsz, seq, d)
```

```python
import functools
import math

import jax
import jax.numpy as jnp
from jax import lax
from jax.experimental import pallas as pl
from jax.experimental.pallas import tpu as pltpu

F32 = jnp.float32
BF16 = jnp.bfloat16

EPS = 1e-6
ROPE_BASE = 10000.0
GRID_W = 64
LOG2E = 1.4426950408889634

DA_HEADS = 8
DA_DIM = 64
DA_COLS = DA_HEADS * 2 * DA_DIM
MLA_HEADS = 8
MLA_NOPE = 128
MLA_ROPE = 64
MLA_V = 128
MLA_Q_RANK = 512
MLA_KV_RANK = 256
MLA_QK_PAD = 256
SSD_HEADDIM = 64
SSD_GROUPS = 8
SSD_STATE = 128
SSD_CONV = 5
SSD_CHUNK = 128

LANE = 128
ROW_TILE = 1024
ATT_TQ = 256
ATT_TK = 1024
ATT_KC = 128
ATT_UNROLL = 8
CONV_ROWS = 256
CONV_BLOCK = 128
CONV_LANES = 256
HALO = 16
MOD_ROWS = 8
VMEM_LIMIT = 56 * 1024 * 1024


def _silu(x):
    return x * (1.0 / (1.0 + jnp.exp(-x)))


def _params(sem):
    return pltpu.CompilerParams(dimension_semantics=sem, vmem_limit_bytes=VMEM_LIMIT)


def _mod_kernel(c_ref, w_ref, b_ref, o_ref):
    s = _silu(c_ref[...]).astype(BF16)
    o_ref[0] = jnp.dot(s, w_ref[0].astype(BF16), preferred_element_type=F32) + b_ref[0]


def _modulation(cvec, mod_w, mod_b):
    depth, d, _ = mod_w.shape
    tn = min(1024, d)
    nj = d // tn
    return pl.pallas_call(
        _mod_kernel,
        grid=(depth, 6, nj),
        in_specs=[
            pl.BlockSpec((MOD_ROWS, d), lambda l, r, j: (0, 0)),
            pl.BlockSpec((1, d, tn), lambda l, r, j: (l, 0, r * nj + j)),
            pl.BlockSpec((1, 1, tn), lambda l, r, j: (l * 6 + r, 0, j)),
        ],
        out_specs=pl.BlockSpec((1, MOD_ROWS, tn), lambda l, r, j: (l * 6 + r, 0, j)),
        out_shape=jax.ShapeDtypeStruct((depth * 6, MOD_ROWS, d), F32),
        compiler_params=_params(("arbitrary", "arbitrary", "arbitrary")),
        name="modulation",
    )(cvec, mod_w, mod_b.reshape(depth * 6, 1, d))


def _cast_kernel(w_ref, o_ref):
    o_ref[...] = w_ref[0].astype(o_ref.dtype)


def _cast_bf16(w, layer, col0=0, ncols=None):
    _, k, n = w.shape
    ncols = n - col0 if ncols is None else ncols
    tn = ncols if ncols <= 2048 else 1024
    tk = min(k, 1024)
    assert ncols % tn == 0 and col0 % tn == 0 and k % tk == 0
    cb0 = col0 // tn
    return pl.pallas_call(
        _cast_kernel,
        grid=(k // tk, ncols // tn),
        in_specs=[pl.BlockSpec((1, tk, tn), lambda i, j: (layer, i, cb0 + j))],
        out_specs=pl.BlockSpec((tk, tn), lambda i, j: (i, j)),
        out_shape=jax.ShapeDtypeStruct((k, ncols), BF16),
        compiler_params=_params(("arbitrary", "arbitrary")),
        name="cast_bf16",
    )(w)


def _linear_kernel(*refs, pre, pre_mod, epi, tiles_per_seq, n_batch, tn, period, rope_cols, w_stacked):
    it = iter(refs)
    x_ref, w_ref = next(it), next(it)
    g_ref = next(it) if pre else None
    sh_ref = next(it) if pre_mod else None
    sc_ref = next(it) if pre_mod else None
    if epi == "softplus":
        bias_ref = next(it)
    elif epi == "rope":
        cos_ref, sin_ref = next(it), next(it)
    elif epi == "gate_res":
        res_ref, gate_ref = next(it), next(it)
    elif epi == "mla_kv":
        kr_ref = next(it)
    o_ref = next(it)
    o2_ref = next(it) if epi == "mla_kv" else None
    h_ref = next(it) if pre else None

    grp = jnp.minimum(pl.program_id(0) // tiles_per_seq, n_batch)

    if pre:
        @pl.when(pl.program_id(1) == 0)
        def _():
            x32 = x_ref[...].astype(F32)
            ms = jnp.mean(x32 * x32, axis=-1, keepdims=True)
            y = x32 * lax.rsqrt(ms + EPS) * g_ref[...]
            if pre_mod:
                y = y * (1.0 + sc_ref[0, pl.ds(grp, 1), :]) + sh_ref[0, pl.ds(grp, 1), :]
            h_ref[...] = y.astype(BF16)
        lhs = h_ref[...]
    else:
        lhs = x_ref[...]

    w = w_ref[0].astype(BF16) if w_stacked else w_ref[...]
    acc = jnp.dot(lhs, w, preferred_element_type=F32)

    if epi == "none":
        o_ref[...] = acc.astype(o_ref.dtype)
    elif epi == "relu2":
        r = jnp.maximum(acc, 0.0)
        o_ref[...] = (r * r).astype(o_ref.dtype)
    elif epi == "softplus":
        v = acc + bias_ref[...]
        o_ref[...] = (jnp.maximum(v, 0.0) + jnp.log(1.0 + jnp.exp(-jnp.abs(v)))).astype(o_ref.dtype)
    elif epi == "rope":
        cos, sin = cos_ref[...], sin_ref[...]
        lane = lax.broadcasted_iota(jnp.int32, cos.shape, 1)
        first = (lane % (2 * 32)) < 32
        for s in range(tn // period):
            a = acc[:, s * period:(s + 1) * period]
            if rope_cols is None or (rope_cols[0] <= s * period < rope_cols[1]
                                     and (s * period - rope_cols[0]) % rope_cols[2] == 0):
                partner = jnp.where(first, pltpu.roll(a, period - 32, 1), pltpu.roll(a, 32, 1))
                a = a * cos + partner * sin
            o_ref[:, s * period:(s + 1) * period] = a.astype(o_ref.dtype)
    elif epi == "gate_res":
        o_ref[...] = res_ref[...] + gate_ref[0, pl.ds(grp, 1), :] * acc
    elif epi == "mla_kv":
        for h in range(MLA_HEADS):
            o_ref[:, h * MLA_QK_PAD:h * MLA_QK_PAD + MLA_NOPE] = acc[:, h * MLA_NOPE:(h + 1) * MLA_NOPE].astype(o_ref.dtype)
            o_ref[:, h * MLA_QK_PAD + MLA_NOPE:(h + 1) * MLA_QK_PAD] = kr_ref[...].astype(o_ref.dtype)
        o2_ref[...] = acc[:, MLA_HEADS * MLA_NOPE:].astype(o2_ref.dtype)


def _linear(x, w, *, tn, out_dtype, lay, name, k=None, x_col_block=0, m_tiles=None, w_layer=None, n=None,
            pre_gain=None, pre_mod=None, epi="none", epi_args=()):
    m = x.shape[0]
    k = x.shape[1] if k is None else k
    n = w.shape[-1] if n is None else n
    tm = ROW_TILE
    mt = m // tm if m_tiles is None else m_tiles
    nt = n // tn
    assert n % tn == 0 and w.shape[-2] == k
    pre = pre_gain is not None
    period = rope_cols = None

    args = [x, w]
    specs = [pl.BlockSpec((tm, k), lambda i, j: (i, x_col_block)),
             pl.BlockSpec((k, tn), lambda i, j: (0, j)) if w_layer is None else
             pl.BlockSpec((1, k, tn), lambda i, j: (w_layer, 0, j))]
    if pre:
        args.append(pre_gain)
        specs.append(pl.BlockSpec((1, k), lambda i, j: (0, 0)))
    if pre_mod is not None:
        mods, shift_idx, scale_idx = pre_mod
        args += [mods, mods]
        specs += [pl.BlockSpec((1, MOD_ROWS, k), lambda i, j: (shift_idx, 0, 0)),
                  pl.BlockSpec((1, MOD_ROWS, k), lambda i, j: (scale_idx, 0, 0))]
    if epi == "softplus":
        (bias,) = epi_args
        args.append(bias)
        specs.append(pl.BlockSpec((1, tn), lambda i, j: (0, j)))
    elif epi == "rope":
        cos, sin, per_col, rope_cols = epi_args
        period = cos.shape[1]
        n_lat, tps = lay["lat_tiles"], lay["tiles_per_seq"]
        tab_map = lambda i, j: (jnp.where(i < n_lat, i % tps, tps) + (j * (tps + 1) if per_col else 0), 0)
        args += [cos, sin]
        specs += [pl.BlockSpec((tm, period), tab_map), pl.BlockSpec((tm, period), tab_map)]
    elif epi == "gate_res":
        res, mods, gate_idx = epi_args
        args += [res, mods]
        specs += [pl.BlockSpec((tm, tn), lambda i, j: (i, j)),
                  pl.BlockSpec((1, MOD_ROWS, tn), lambda i, j: (gate_idx, 0, j))]
    out_specs = pl.BlockSpec((tm, tn), lambda i, j: (i, j))
    out_shape = jax.ShapeDtypeStruct((mt * tm, n), out_dtype)
    if epi == "mla_kv":
        k_rope, kr_col = epi_args
        assert nt == 1 and n == MLA_HEADS * (MLA_NOPE + MLA_V)
        args.append(k_rope)
        specs.append(pl.BlockSpec((tm, LANE), lambda i, j: (i, kr_col)))
        out_specs = [pl.BlockSpec((tm, MLA_HEADS * MLA_QK_PAD), lambda i, j: (i, 0)),
                     pl.BlockSpec((tm, MLA_HEADS * MLA_V), lambda i, j: (i, 0))]
        out_shape = [jax.ShapeDtypeStruct((mt * tm, MLA_HEADS * MLA_QK_PAD), out_dtype),
                     jax.ShapeDtypeStruct((mt * tm, MLA_HEADS * MLA_V), out_dtype)]

    kern = functools.partial(_linear_kernel, pre=pre, pre_mod=pre_mod is not None, epi=epi,
                             tiles_per_seq=lay["tiles_per_seq"], n_batch=lay["batch"], tn=tn, period=period,
                             rope_cols=rope_cols, w_stacked=w_layer is not None)
    return pl.pallas_call(
        kern,
        grid=(mt, nt),
        in_specs=specs,
        out_specs=out_specs,
        out_shape=out_shape,
        scratch_shapes=[pltpu.VMEM((tm, k), BF16)] if pre else [],
        compiler_params=_params(("arbitrary", "arbitrary")),
        name=name,
    )(*args)


def _attn_kernel(lam_ref, gsub_ref, qa_ref, qm_ref, kac_ref, vac_ref, kmc_ref, vmc_ref,
                 ka_ref, va_ref, km_ref, vm_ref, o_ref, qta_ref, qtm_ref, m_ref, l_ref, acc_ref,
                 *, lambda_init, n_steps, n_lat_q):
    qi, t = pl.program_id(0), pl.program_id(1)
    tq = qa_ref.shape[0]
    ctx_len, tk, kc = kac_ref.shape[0], ka_ref.shape[0], ATT_KC

    @pl.when(t == 0)
    def _init():
        m_ref[...] = jnp.full(m_ref.shape, -jnp.inf, F32)
        l_ref[...] = jnp.zeros(l_ref.shape, F32)
        acc_ref[...] = jnp.zeros(acc_ref.shape, F32)
        first = lax.broadcasted_iota(jnp.int32, (LANE, tq), 0) < DA_DIM
        for h in range(DA_HEADS):
            qt = qa_ref[:, h * LANE:(h + 1) * LANE].astype(F32).T
            qta_ref[2 * h] = jnp.where(first, qt, 0.0).astype(BF16)
            qta_ref[2 * h + 1] = jnp.where(first, 0.0, qt).astype(BF16)
        for h in range(MLA_HEADS):
            qtm_ref[h] = qm_ref[:, h * MLA_QK_PAD:(h + 1) * MLA_QK_PAD].astype(F32).T.astype(BF16)

    def update(idx, qt, k, v):
        s = jnp.dot(k, qt, preferred_element_type=F32)
        m_prev = m_ref[idx]
        m_next = jnp.maximum(m_prev, jnp.max(s, axis=0, keepdims=True))
        alpha = jnp.exp2(m_prev - m_next)
        p = jnp.exp2(s - m_next)
        l_ref[idx] = alpha * l_ref[idx] + jnp.sum(p, axis=0, keepdims=True)
        pv = lax.dot_general(v, p.astype(BF16), (((0,), (0,)), ((), ())), preferred_element_type=F32)
        acc_ref[idx] = alpha * acc_ref[idx] + pv
        m_ref[idx] = m_next

    def chunk(ka, va, km, vm, r0):
        rows = pl.ds(r0, kc)
        for h in range(DA_HEADS):
            k = ka[rows, h * LANE:(h + 1) * LANE]
            v = va[rows, h * LANE:(h + 1) * LANE]
            for c in range(2):
                update(2 * h + c, qta_ref[2 * h + c], k, v)
        for h in range(MLA_HEADS):
            update(2 * DA_HEADS + h, qtm_ref[h], km[rows, h * MLA_QK_PAD:(h + 1) * MLA_QK_PAD],
                   vm[rows, h * MLA_V:(h + 1) * MLA_V])

    @pl.when(t == 0)
    def _ctx():
        def body(r, carry):
            chunk(kac_ref, vac_ref, kmc_ref, vmc_ref, pl.multiple_of(r * kc, kc))
            return carry
        lax.fori_loop(0, ctx_len // kc, body, 0, unroll=True)

    @pl.when(qi < n_lat_q)
    def _lat():
        def body(r, carry):
            chunk(ka_ref, va_ref, km_ref, vm_ref, pl.multiple_of(r * kc, kc))
            return carry
        lax.fori_loop(0, tk // kc, body, 0, unroll=ATT_UNROLL)

    @pl.when(t == n_steps - 1)
    def _finish():
        lv = lam_ref[...]
        lam = (jnp.exp(jnp.sum(lv[0:1] * lv[1:2], axis=1, keepdims=True))
               - jnp.exp(jnp.sum(lv[2:3] * lv[3:4], axis=1, keepdims=True)) + lambda_init)
        for h in range(DA_HEADS):
            o = acc_ref[2 * h] / l_ref[2 * h] - lam * (acc_ref[2 * h + 1] / l_ref[2 * h + 1])
            ms = jnp.mean(o * o, axis=0, keepdims=True)
            o = (o * lax.rsqrt(ms + EPS)).T * gsub_ref[...] * (1.0 - lambda_init)
            o_ref[:, h * LANE:(h + 1) * LANE] = o.astype(o_ref.dtype)
        for h in range(MLA_HEADS):
            idx = 2 * DA_HEADS + h
            o_ref[:, DA_COLS + h * MLA_V:DA_COLS + (h + 1) * MLA_V] = (acc_ref[idx] / l_ref[idx]).T.astype(o_ref.dtype)


def _attention(qkv_a, q_m, k_m, v_m, lam_vecs, g_sub, *, lay, lambda_init):
    m = qkv_a.shape[0]
    tq, tk = ATT_TQ, ATT_TK
    s_len, ctx_len, bsz = lay["seq"], lay["ctx"], lay["batch"]
    n_lat_q = bsz * s_len // tq
    n_ctx_q = (m - bsz * s_len) // tq
    q_per_seq, q_per_ctx = s_len // tq, ctx_len // tq
    n_steps = s_len // tk
    ctx_base = bsz * s_len // ctx_len

    def batch_of(qi):
        return jnp.where(qi < n_lat_q, qi // q_per_seq, jnp.minimum((qi - n_lat_q) // q_per_ctx, bsz - 1))

    ctx_row = lambda col: (lambda qi, t: (ctx_base + batch_of(qi), col))
    lat_row = lambda col: (lambda qi, t: (batch_of(qi) * n_steps + jnp.where(qi < n_lat_q, t, 0), col))

    n_state = 2 * DA_HEADS + MLA_HEADS
    kern = functools.partial(_attn_kernel, lambda_init=lambda_init, n_steps=n_steps, n_lat_q=n_lat_q)
    return pl.pallas_call(
        kern,
        grid=(n_lat_q + n_ctx_q, n_steps),
        in_specs=[
            pl.BlockSpec((4, DA_DIM), lambda qi, t: (0, 0)),
            pl.BlockSpec((1, 2 * DA_DIM), lambda qi, t: (0, 0)),
            pl.BlockSpec((tq, DA_COLS), lambda qi, t: (qi, 0)),
            pl.BlockSpec((tq, MLA_HEADS * MLA_QK_PAD), lambda qi, t: (qi, 0)),
            pl.BlockSpec((ctx_len, DA_COLS), ctx_row(1)),
            pl.BlockSpec((ctx_len, DA_COLS), ctx_row(2)),
            pl.BlockSpec((ctx_len, MLA_HEADS * MLA_QK_PAD), ctx_row(0)),
            pl.BlockSpec((ctx_len, MLA_HEADS * MLA_V), ctx_row(0)),
            pl.BlockSpec((tk, DA_COLS), lat_row(1)),
            pl.BlockSpec((tk, DA_COLS), lat_row(2)),
            pl.BlockSpec((tk, MLA_HEADS * MLA_QK_PAD), lat_row(0)),
            pl.BlockSpec((tk, MLA_HEADS * MLA_V), lat_row(0)),
        ],
        out_specs=pl.BlockSpec((tq, DA_COLS + MLA_HEADS * MLA_V), lambda qi, t: (qi, 0)),
        out_shape=jax.ShapeDtypeStruct((m, DA_COLS + MLA_HEADS * MLA_V), BF16),
        scratch_shapes=[
            pltpu.VMEM((2 * DA_HEADS, LANE, tq), BF16),
            pltpu.VMEM((MLA_HEADS, MLA_QK_PAD, tq), BF16),
            pltpu.VMEM((n_state, 1, tq), F32),
            pltpu.VMEM((n_state, 1, tq), F32),
            pltpu.VMEM((n_state, LANE, tq), F32),
        ],
        compiler_params=_params(("arbitrary", "arbitrary")),
        name="attention",
    )(lam_vecs, g_sub, qkv_a, q_m, qkv_a, qkv_a, k_m, v_m, qkv_a, qkv_a, k_m, v_m)


def _conv_kernel(xm_ref, xp_ref, xn_ref, w_ref, b_ref, s_ref, o_ref, *, tm, seq, ctx, n_lat_rows):
    row0 = pl.program_id(0) * tm
    in_lat = row0 < n_lat_rows
    local = jnp.where(in_lat, row0 % seq, (row0 - n_lat_rows) % ctx)
    seq_len = jnp.where(in_lat, seq, ctx)
    keep_prev = (local != 0).astype(BF16)
    keep_next = (local + tm != seq_len).astype(BF16)
    blk = CONV_BLOCK
    for r in range(tm // blk):
        prev = xp_ref[...] * keep_prev if r == 0 else xm_ref[r * blk - HALO:r * blk, :]
        nxt = xn_ref[...] * keep_next if r == tm // blk - 1 else xm_ref[(r + 1) * blk:(r + 1) * blk + HALO, :]
        x_ext = jnp.concatenate([prev, xm_ref[r * blk:(r + 1) * blk, :], nxt], axis=0)
        for cs in range(0, x_ext.shape[1], CONV_LANES):
            cols = slice(cs, cs + CONV_LANES)
            acc = jnp.broadcast_to(b_ref[:, cols], (blk, CONV_LANES))
            for kk in range(SSD_CONV):
                acc = acc + w_ref[kk:kk + 1, cols] * jnp.dot(s_ref[kk], x_ext[:, cols], preferred_element_type=F32)
            o_ref[r * blk:(r + 1) * blk, cols] = _silu(acc).astype(o_ref.dtype)


def _ssd_conv(zx, conv_w, conv_b, *, lay, col0):
    m = zx.shape[0]
    c = conv_w.shape[1]
    tm, tc = CONV_ROWS, 1024
    hb = tm // HALO
    last = m // HALO - 1
    cb0 = col0 // tc
    rows = lax.broadcasted_iota(jnp.int32, (SSD_CONV, CONV_BLOCK, CONV_BLOCK + 2 * HALO), 1)
    cols = lax.broadcasted_iota(jnp.int32, (SSD_CONV, CONV_BLOCK, CONV_BLOCK + 2 * HALO), 2)
    taps = lax.broadcasted_iota(jnp.int32, (SSD_CONV, CONV_BLOCK, CONV_BLOCK + 2 * HALO), 0)
    shifts = (cols == rows + HALO - SSD_CONV // 2 + taps).astype(BF16)
    kern = functools.partial(_conv_kernel, tm=tm, seq=lay["seq"], ctx=lay["ctx"], n_lat_rows=lay["batch"] * lay["seq"])
    return pl.pallas_call(
        kern,
        grid=(m // tm, c // tc),
        in_specs=[
            pl.BlockSpec((tm, tc), lambda i, j: (i, cb0 + j)),
            pl.BlockSpec((HALO, tc), lambda i, j: (jnp.maximum(i * hb - 1, 0), cb0 + j)),
            pl.BlockSpec((HALO, tc), lambda i, j: (jnp.minimum((i + 1) * hb, last), cb0 + j)),
            pl.BlockSpec((SSD_CONV, tc), lambda i, j: (0, j)),
            pl.BlockSpec((1, tc), lambda i, j: (0, j)),
            pl.BlockSpec(shifts.shape, lambda i, j: (0, 0, 0)),
        ],
        out_specs=pl.BlockSpec((tm, tc), lambda i, j: (i, j)),
        out_shape=jax.ShapeDtypeStruct((m, c), BF16),
        compiler_params=_params(("arbitrary", "arbitrary")),
        name="ssd_conv",
    )(zx, zx, zx, conv_w, conv_b, shifts)


def _ssd_scan_kernel(*refs, direction, finish, n_heads, n_batch, pad_rows):
    it = iter(refs)
    alog_ref, x_ref, b_ref, c_ref, dt_ref = (next(it) for _ in range(5))
    if finish:
        yf_ref, z_ref, dsk_ref, gn_ref = (next(it) for _ in range(4))
    o_ref = next(it)
    state_ref, lhs_ref, rhs_ref, bw_ref = (next(it) for _ in range(4))
    y_ref = next(it) if finish else o_ref

    L, N = SSD_CHUNK, SSD_STATE
    assert L == N == LANE
    heads_per_group = n_heads // SSD_GROUPS
    pairs_per_group = heads_per_group // 2

    def main():
        @pl.when(pl.program_id(1) == 0)
        def _():
            state_ref[...] = jnp.zeros(state_ref.shape, F32)

        a_rate = -jnp.exp(alog_ref[...])
        dt = dt_ref[...]
        ri = lax.broadcasted_iota(jnp.int32, (L, L), 0)
        ci = lax.broadcasted_iota(jnp.int32, (L, L), 1)
        tri = (ri >= ci) if direction == 0 else (ri <= ci)
        dta = dt * a_rate
        hi = dta.astype(BF16)
        r1 = dta - hi.astype(F32)
        mid = r1.astype(BF16)
        lo = (r1 - mid.astype(F32)).astype(BF16)
        tri_b = tri.astype(BF16)
        a = (jnp.dot(tri_b, hi, preferred_element_type=F32) + jnp.dot(tri_b, mid, preferred_element_type=F32)
             + jnp.dot(tri_b, lo, preferred_element_type=F32))
        a = a * LOG2E
        a_t = a.T
        dt_t = dt.T
        la_t = a_t - jnp.log2(dt_t)
        last = L - 1 if direction == 0 else 0
        first = lax.broadcasted_iota(jnp.int32, (1, LANE), 1) < SSD_HEADDIM

        for g in range(SSD_GROUPS):
            bg = b_ref[:, g * N:(g + 1) * N]
            cg = c_ref[:, g * N:(g + 1) * N]
            cb = lax.dot_general(cg, bg, (((1,), (1,)), ((), ())), preferred_element_type=F32)
            cb_b = cb.astype(BF16)
            bg_t = bg.astype(F32).T.astype(BF16)
            for pp in range(pairs_per_group):
                p = g * pairs_per_group + pp
                xp = x_ref[:, p * LANE:(p + 1) * LANE]
                h_t = state_ref[p]
                dec = []
                for kk in range(2):
                    c = direction * n_heads + 2 * p + kk
                    tot = a_t[c:c + 1, last:last + 1]
                    a_b = jnp.broadcast_to(a[:, c:c + 1], (L, L))
                    seg_dt = jnp.exp2(jnp.where(tri, a_b - la_t[c:c + 1, :], -jnp.inf))
                    lhs_ref[p, :, kk * L:(kk + 1) * L] = cb_b * seg_dt.astype(BF16)
                    lhs_ref[p, :, (2 + kk) * L:(3 + kk) * L] = cg * jnp.exp2(a_b).astype(BF16)
                    bw_ref[p, :, kk * L:(kk + 1) * L] = bg_t * (jnp.exp2(tot - a_t[c:c + 1, :]) * dt_t[c:c + 1, :]).astype(BF16)
                    dec.append(jnp.exp2(tot))
                zero_x = jnp.zeros_like(xp)
                h_b = h_t.astype(BF16)
                zero_h = jnp.zeros_like(h_b)
                rhs_ref[p, 0:L] = jnp.where(first, xp, zero_x)
                rhs_ref[p, L:2 * L] = jnp.where(first, zero_x, xp)
                rhs_ref[p, 2 * L:3 * L] = jnp.where(first, h_b, zero_h)
                rhs_ref[p, 3 * L:4 * L] = jnp.where(first, zero_h, h_b)
                y = jnp.dot(lhs_ref[p], rhs_ref[p], preferred_element_type=F32)
                y_ref[:, p * LANE:(p + 1) * LANE] = y.astype(y_ref.dtype)
                upd = jnp.dot(bw_ref[p], rhs_ref[p, 0:2 * L], preferred_element_type=F32)
                state_ref[p] = h_t * jnp.where(first, dec[0], dec[1]) + upd

        if finish:
            gw = n_heads * SSD_HEADDIM // SSD_GROUPS
            for g in range(SSD_GROUPS):
                sl = slice(g * gw, (g + 1) * gw)
                v = yf_ref[:, sl].astype(F32) + y_ref[:, sl] + dsk_ref[:, sl] * x_ref[:, sl].astype(F32)
                v = v * _silu(z_ref[:, sl].astype(F32))
                ms = jnp.mean(v * v, axis=1, keepdims=True)
                o_ref[:, sl] = (v * lax.rsqrt(ms + EPS) * gn_ref[:, sl]).astype(o_ref.dtype)

    if pad_rows:
        pl.when(pl.program_id(0) < n_batch)(main)

        @pl.when(pl.program_id(0) >= n_batch)
        def _():
            o_ref[...] = jnp.zeros(o_ref.shape, o_ref.dtype)
    else:
        main()


def _ssd_scan(xbc, dt, a_log, *, lay, direction, finish_args=None):
    m = xbc.shape[0]
    L = SSD_CHUNK
    gn = SSD_GROUPS * SSD_STATE
    inner = xbc.shape[1] - 2 * gn
    n_heads = inner // SSD_HEADDIM
    bsz = lay["batch"]
    ncl, ncc = lay["seq"] // L, lay["ctx"] // L
    ctx_base = bsz * ncl
    finish = finish_args is not None

    n_real = bsz * (ncl + ncc)
    n_pad = (m // L - n_real) if finish else 0

    def chunk(b, t):
        if direction == 0:
            real = jnp.where(t < ncc, ctx_base + b * ncc + t, b * ncl + (t - ncc))
        else:
            real = jnp.where(t < ncc, ctx_base + b * ncc + (ncc - 1 - t), b * ncl + (ncl - 1 - (t - ncc)))
        if n_pad:
            return jnp.where(b < bsz, real, n_real + jnp.minimum(t, n_pad - 1))
        return real

    row = lambda b, t: (chunk(b, t), 0)
    args = [a_log, xbc, xbc, xbc, dt]
    specs = [
        pl.BlockSpec((1, 2 * n_heads), lambda b, t: (0, 0)),
        pl.BlockSpec((L, inner), row),
        pl.BlockSpec((L, gn), lambda b, t: (chunk(b, t), inner // gn)),
        pl.BlockSpec((L, gn), lambda b, t: (chunk(b, t), inner // gn + 1)),
        pl.BlockSpec((L, 2 * n_heads), row),
    ]
    n_pairs = n_heads // 2
    scratch = [pltpu.VMEM((n_pairs, SSD_STATE, LANE), F32),
               pltpu.VMEM((n_pairs, L, 4 * L), BF16),
               pltpu.VMEM((n_pairs, 4 * L, LANE), BF16),
               pltpu.VMEM((n_pairs, SSD_STATE, 2 * L), BF16)]
    if finish:
        y_f, zx, dsk, g_norm = finish_args
        args += [y_f, zx, dsk, g_norm]
        specs += [pl.BlockSpec((L, inner), row), pl.BlockSpec((L, inner), row),
                  pl.BlockSpec((1, inner), lambda b, t: (0, 0)), pl.BlockSpec((1, inner), lambda b, t: (0, 0))]
        scratch.append(pltpu.VMEM((L, inner), F32))
    kern = functools.partial(_ssd_scan_kernel, direction=direction, finish=finish, n_heads=n_heads,
                             n_batch=bsz, pad_rows=n_pad > 0)
    return pl.pallas_call(
        kern,
        grid=(bsz + (1 if n_pad else 0), ncc + ncl),
        in_specs=specs,
        out_specs=pl.BlockSpec((L, inner), row),
        out_shape=jax.ShapeDtypeStruct((m, inner), BF16),
        scratch_shapes=scratch,
        compiler_params=_params(("arbitrary", "arbitrary")),
        name="ssd_scan_bwd_finish" if finish else "ssd_scan_fwd",
    )(*args)


def _final_norm_kernel(x_ref, g_ref, o_ref):
    x = x_ref[...]
    ms = jnp.mean(x * x, axis=-1, keepdims=True)
    o_ref[...] = x * lax.rsqrt(ms + EPS) * g_ref[...]


def _final_norm(x, g, rows):
    d = x.shape[1]
    return pl.pallas_call(
        _final_norm_kernel,
        grid=(rows // ROW_TILE,),
        in_specs=[pl.BlockSpec((ROW_TILE, d), lambda i: (i, 0)), pl.BlockSpec((1, d), lambda i: (0, 0))],
        out_specs=pl.BlockSpec((ROW_TILE, d), lambda i: (i, 0)),
        out_shape=jax.ShapeDtypeStruct((rows, d), F32),
        compiler_params=_params(("arbitrary",)),
        name="final_norm",
    )(x, g)


def _rope_tables(seq, pattern, scale=1.0):
    rows = seq // GRID_W
    pos_r = jnp.broadcast_to(jnp.arange(rows, dtype=F32)[:, None], (rows, GRID_W)).reshape(seq)
    pos_c = jnp.broadcast_to(jnp.arange(GRID_W, dtype=F32)[None, :], (rows, GRID_W)).reshape(seq)
    n = DA_DIM // 4
    inv = ROPE_BASE ** (-jnp.arange(n, dtype=F32) / n)
    ang = jnp.concatenate([pos_r[:, None] * inv, pos_c[:, None] * inv], axis=-1)
    cos, sin = jnp.cos(ang), jnp.sin(ang)
    one, zero = jnp.ones_like(cos), jnp.zeros_like(cos)
    cos_t = jnp.concatenate([{"a": cos, "b": cos, "i": one}[c] for c in pattern], axis=1)
    sin_t = jnp.concatenate([{"a": -sin, "b": sin, "i": zero}[c] for c in pattern], axis=1)
    pad_c = jnp.ones((ROW_TILE, cos_t.shape[1]), F32)
    pad_s = jnp.zeros((ROW_TILE, cos_t.shape[1]), F32)
    return jnp.concatenate([cos_t, pad_c], axis=0) * scale, jnp.concatenate([sin_t, pad_s], axis=0) * scale


def kernel(x, c, ctx, c_ctx, mod_w, mod_b, norm_mix, norm_ffn, ffn_w1, ffn_w2, attn_w_in, mla_g_q, mla_w_uq, mla_g_kv, mla_w_ukv, da_lam_q1, da_lam_k1, da_lam_q2, da_lam_k2, da_g_sub, attn_w_out, ssd_w_in, ssd_conv_w, ssd_conv_b, ssd_dt_bias, ssd_a_log, ssd_d, ssd_g_norm, ssd_w_out, final_g):
    bsz, seq, d = x.shape
    ctx_len = ctx.shape[1]
    depth = mod_w.shape[0]
    assert seq % ROW_TILE == 0 and bsz + 1 <= MOD_ROWS
    assert seq % ATT_TK == 0 and ctx_len % ATT_TQ == 0 and ctx_len % ATT_KC == 0 and seq % GRID_W == 0
    assert seq % CONV_ROWS == 0 and ctx_len % CONV_ROWS == 0 and seq % SSD_CHUNK == 0 and ctx_len % SSD_CHUNK == 0
    lay = dict(batch=bsz, seq=seq, ctx=ctx_len, tiles_per_seq=seq // ROW_TILE, lat_tiles=bsz * seq // ROW_TILE)
    n_lat = bsz * seq
    lat_tiles = n_lat // ROW_TILE

    n_pad = -(n_lat + bsz * ctx_len) % ROW_TILE
    xs = jnp.concatenate([x.reshape(n_lat, d), ctx.reshape(bsz * ctx_len, d), jnp.zeros((n_pad, d), F32)], axis=0)
    cvec = jnp.zeros((MOD_ROWS, d), F32).at[:bsz].set(c).at[bsz].set(c_ctx)
    mods = _modulation(cvec, mod_w, mod_b)

    cq, sq = _rope_tables(seq, "abab", DA_DIM ** -0.5 * LOG2E)
    ck, sk = _rope_tables(seq, "abab")
    cv, sv = _rope_tables(seq, "iiii")
    cos_da, sin_da = jnp.concatenate([cq, ck, cv], axis=0), jnp.concatenate([sq, sk, sv], axis=0)
    cos_kr, sin_kr = _rope_tables(seq, "abii")

    inner = ssd_w_out.shape[1]
    n_heads = inner // SSD_HEADDIM

    for l in range(depth):
        last = l == depth - 1
        i = l // 2
        mix_tiles = None
        out_tiles = lat_tiles if last else None
        pre_mix = dict(pre_gain=norm_mix[l][None, :], pre_mod=(mods, l * 6 + 0, l * 6 + 1))
        if l % 2 == 0:
            lambda_init = 0.8 - 0.6 * math.exp(-0.3 * l)
            c0, n_c = 3 * DA_COLS, MLA_Q_RANK + MLA_KV_RANK
            w_ckr = jnp.pad(attn_w_in[i][:, c0:], ((0, 0), (0, LANE - MLA_ROPE))).astype(BF16)
            mla_scale = (MLA_NOPE + MLA_ROPE) ** -0.5 * LOG2E
            w_uq = jnp.pad((mla_w_uq[i] * mla_scale).reshape(MLA_Q_RANK, MLA_HEADS, MLA_NOPE + MLA_ROPE),
                           ((0, 0), (0, 0), (0, MLA_QK_PAD - MLA_NOPE - MLA_ROPE)))
            w_uq = w_uq.reshape(MLA_Q_RANK, MLA_HEADS * MLA_QK_PAD).astype(BF16)
            w_ukv = mla_w_ukv[i].reshape(MLA_KV_RANK, MLA_HEADS, MLA_NOPE + MLA_V)
            w_ukv = jnp.concatenate([w_ukv[:, :, :MLA_NOPE].reshape(MLA_KV_RANK, -1),
                                     w_ukv[:, :, MLA_NOPE:].reshape(MLA_KV_RANK, -1)], axis=1).astype(BF16)

            qkv_a = _linear(xs, attn_w_in, w_layer=i, n=c0, tn=1024, out_dtype=BF16, lay=lay, name="attn_in_qkv", m_tiles=mix_tiles,
                            epi="rope", epi_args=(cos_da, sin_da, True, None), **pre_mix)
            cq_ckv = _linear(xs, w_ckr, tn=n_c + LANE, out_dtype=F32, lay=lay, name="attn_in_c", m_tiles=mix_tiles,
                             epi="rope", epi_args=(cos_kr, sin_kr, False, (n_c, n_c + LANE, LANE)), **pre_mix)
            q_m = _linear(cq_ckv, w_uq, tn=MLA_HEADS * MLA_QK_PAD, out_dtype=BF16, lay=lay, name="mla_uq",
                          k=MLA_Q_RANK, x_col_block=0, pre_gain=mla_g_q[i][None, :],
                          epi="rope", epi_args=(cos_kr, sin_kr, False, (MLA_NOPE, MLA_HEADS * MLA_QK_PAD, MLA_QK_PAD)))
            k_m, v_m = _linear(cq_ckv, w_ukv, tn=MLA_HEADS * (MLA_NOPE + MLA_V), out_dtype=BF16, lay=lay, name="mla_ukv",
                               k=MLA_KV_RANK, x_col_block=MLA_Q_RANK // MLA_KV_RANK, pre_gain=mla_g_kv[i][None, :],
                               epi="mla_kv", epi_args=(cq_ckv, n_c // LANE))
            lam_vecs = jnp.stack([da_lam_q1[i], da_lam_k1[i], da_lam_q2[i], da_lam_k2[i]]).astype(F32)
            o_att = _attention(qkv_a, q_m, k_m, v_m, lam_vecs, da_g_sub[i][None, :].astype(F32),
                               lay=lay, lambda_init=lambda_init)
            xs = _linear(o_att, _cast_bf16(attn_w_out, i), tn=1024, out_dtype=F32, lay=lay, name="attn_out",
                         m_tiles=out_tiles, epi="gate_res", epi_args=(xs, mods, l * 6 + 2))
        else:
            n_zx = inner + inner + 2 * SSD_GROUPS * SSD_STATE
            zx = _linear(xs, ssd_w_in, w_layer=i, n=n_zx, tn=1024, out_dtype=BF16, lay=lay, name="ssd_in_zx",
                         m_tiles=mix_tiles, **pre_mix)
            dt = _linear(xs, _cast_bf16(ssd_w_in, i, n_zx, 2 * n_heads), tn=2 * n_heads, out_dtype=F32, lay=lay, name="ssd_in_dt",
                         m_tiles=mix_tiles, epi="softplus", epi_args=(ssd_dt_bias[i].reshape(1, -1).astype(F32),),
                         **pre_mix)
            xbc = _ssd_conv(zx, ssd_conv_w[i].astype(F32), ssd_conv_b[i][None, :].astype(F32), lay=lay, col0=inner)
            a_log = ssd_a_log[i].reshape(1, -1).astype(F32)
            y_f = _ssd_scan(xbc, dt, a_log, lay=lay, direction=0)
            dsk = jnp.repeat(ssd_d[i].astype(F32), SSD_HEADDIM)[None, :]
            y = _ssd_scan(xbc, dt, a_log, lay=lay, direction=1,
                          finish_args=(y_f, zx, dsk, ssd_g_norm[i][None, :].astype(F32)))
            xs = _linear(y, _cast_bf16(ssd_w_out, i), tn=1024, out_dtype=F32, lay=lay, name="ssd_out",
                         m_tiles=out_tiles, epi="gate_res", epi_args=(xs, mods, l * 6 + 2))
        hid = _linear(xs, ffn_w1, w_layer=l, tn=1024, out_dtype=BF16, lay=lay, name="ffn_up", m_tiles=out_tiles,
                      pre_gain=norm_ffn[l][None, :], pre_mod=(mods, l * 6 + 3, l * 6 + 4), epi="relu2")
        xs = _linear(hid, _cast_bf16(ffn_w2, l), tn=256, out_dtype=F32, lay=lay, name="ffn_down", m_tiles=out_tiles,
                     epi="gate_res", epi_args=(xs, mods, l * 6 + 5))

    return _final_norm(xs, final_g[None, :].astype(F32), n_lat).reshape(bsz, seq, d)
```
